```python
import math
import jax, jax.numpy as jnp
from jax import lax
import numpy as np

D_MODEL = 2048
BATCH = 8
SEQ = 2048
DEPTH = 1
DEC_BATCH = 128
DEC_SEQ = 1
PAST_LEN = 2048
PAGE_SIZE = 128

N_HEADS = 16
HEAD_DIM = 64
ATTN_E = 2 * HEAD_DIM
ATTN_WIDTH = N_HEADS * ATTN_E
CONV_DIM = 1024
CONV_WIDTH = 31
D_FF = -(-8 * D_MODEL // (3 * 256)) * 256
PLE_DIM = 256
Q_BLOCK = 128
EPS = 1e-6
NEG_INF = -1e30
ATTN_SCALE = HEAD_DIM ** -0.5
IN_COLS = 3 * ATTN_WIDTH + 2 * CONV_DIM + 2 * D_MODEL

kernel_name = "hybrid_diffattn_conformer_conv_step"


def _rmsnorm(x, g):
    xf = x.astype(jnp.float32)
    y = xf * lax.rsqrt(jnp.mean(xf * xf, axis=-1, keepdims=True) + EPS)
    return (y * g.astype(jnp.float32)).astype(x.dtype)


def _layernorm(x, g, b):
    xf = x.astype(jnp.float32)
    mu = jnp.mean(xf, axis=-1, keepdims=True)
    var = jnp.mean(jnp.square(xf - mu), axis=-1, keepdims=True)
    y = (xf - mu) * lax.rsqrt(var + EPS)
    return (y * g.astype(jnp.float32) + b.astype(jnp.float32)).astype(x.dtype)


def _alibi_slopes():
    return 2.0 ** (-8.0 * jnp.arange(1, N_HEADS + 1, dtype=jnp.float32) / N_HEADS)


def _alibi_logits(q, k, q_pos, k_pos, slopes):
    B, Tq = q.shape[:2]
    Tk = k.shape[1]
    q2 = q.reshape(B, Tq, N_HEADS, 2, HEAD_DIM)
    k2 = k.reshape(B, Tk, N_HEADS, 2, HEAD_DIM)
    s = jnp.einsum('bqhmd,bkhmd->bmhqk', q2, k2, preferred_element_type=jnp.float32) * ATTN_SCALE
    dist = q_pos[:, None] - k_pos[None, :]
    s = s - slopes[:, None, None] * dist.astype(jnp.float32)
    return jnp.where(dist >= 0, s, NEG_INF)


def _diff_weights(logits, lam):
    a = jax.nn.softmax(logits, axis=-1)
    return a[:, 0] - lam * a[:, 1]


def _prompt_attend(q, k, v, lam, slopes):
    B, T = q.shape[:2]
    nb = T // Q_BLOCK
    qb = q.reshape(B, nb, Q_BLOCK, N_HEADS, ATTN_E).transpose(1, 0, 2, 3, 4)
    pos = jnp.arange(T)
    pb = pos.reshape(nb, Q_BLOCK)

    def block(args):
        qi, pi = args
        w = _diff_weights(_alibi_logits(qi, k, pi, pos, slopes), lam)
        return jnp.einsum('bhqk,bkhe->bqhe', w.astype(v.dtype), v)

    o = lax.map(block, (qb, pb))
    return o.transpose(1, 0, 2, 3, 4).reshape(B, T, N_HEADS, ATTN_E)


def _sample_attend(q, k, v, lam, slopes, k_past, v_past):
    P = k_past.shape[1]
    Tq = q.shape[1]
    q_pos = P + jnp.arange(Tq)
    logits = jnp.concatenate([
        _alibi_logits(q, k_past, q_pos, jnp.arange(P), slopes),
        _alibi_logits(q, k, q_pos, q_pos, slopes)], axis=-1)
    w = _diff_weights(logits, lam)
    return (jnp.einsum('bhqk,bkhe->bqhe', w[..., :P].astype(v.dtype), v_past)
            + jnp.einsum('bhqk,bkhe->bqhe', w[..., P:].astype(v.dtype), v))


def _depthwise_causal(full, w, b):
    y = lax.conv_general_dilated(full, w[:, None, :].astype(full.dtype), window_strides=(1,),
                                 padding='VALID', dimension_numbers=('NWC', 'WIO', 'NWC'),
                                 feature_group_count=CONV_DIM)
    return y + b


def _layer(x, pe, conv_hist, attend, lam_init, lw):
    B, T = x.shape[:2]
    h = _rmsnorm(x, lw['g_pre_mix'])
    z = h @ lw['w_in']
    q, k, v, u_in, gates = jnp.split(
        z, [ATTN_WIDTH, 2 * ATTN_WIDTH, 3 * ATTN_WIDTH, 3 * ATTN_WIDTH + 2 * CONV_DIM], axis=-1)
    q = q.reshape(B, T, N_HEADS, ATTN_E)
    k = k.reshape(B, T, N_HEADS, ATTN_E)
    v = v.reshape(B, T, N_HEADS, ATTN_E)
    lam = (jnp.exp(jnp.sum(lw['lambda_q1'].astype(jnp.float32) * lw['lambda_k1'].astype(jnp.float32)))
           - jnp.exp(jnp.sum(lw['lambda_q2'].astype(jnp.float32) * lw['lambda_k2'].astype(jnp.float32)))
           + lam_init)
    o = attend(q, k, v, lam)
    o = _rmsnorm(o, lw['g_subln']) * (1.0 - lam_init)
    a = o.reshape(B, T, ATTN_WIDTH) @ lw['w_attn_out']
    ua, ub = jnp.split(u_in, 2, axis=-1)
    u = ua * jax.nn.sigmoid(ub)
    full = jnp.concatenate([conv_hist.astype(u.dtype), u], axis=1)
    c = _depthwise_causal(full, lw['conv_w'], lw['conv_b'])
    c = jax.nn.silu(_layernorm(c, lw['g_conv_norm'], lw['b_conv_norm'])) @ lw['w_conv_out']
    g_a, g_c = jnp.split(jax.nn.sigmoid(gates), 2, axis=-1)
    mix = (g_a * a + g_c * c) @ lw['w_out']
    x = x + _rmsnorm(mix, lw['g_post_mix'])
    f = _rmsnorm(x, lw['g_pre_ffn'])
    f = (jax.nn.silu(f @ lw['w_ffn_gate']) * (f @ lw['w_ffn_up'])) @ lw['w_ffn_down']
    x = x + _rmsnorm(f, lw['g_post_ffn'])
    x = x + jax.nn.sigmoid(x @ lw['w_ple_gate']) * (pe @ lw['w_ple_proj'])
    return x, k, v, full[:, -(CONV_WIDTH - 1):]


def setup_inputs(seed: int = 0) -> dict:
    key = jax.random.key(seed)
    ks = jax.random.split(key, 32)
    f32 = jnp.float32
    n_pages = PAST_LEN // PAGE_SIZE
    n_pool = (DEC_BATCH * n_pages * 5) // 4

    def nrm(k, shape, scale):
        return jax.random.normal(k, shape, f32) * scale

    def gain(k, shape):
        return 1.0 + 0.05 * jax.random.normal(k, shape, f32)

    perm = jax.random.permutation(ks[5], n_pool)[:DEC_BATCH * n_pages]
    return {
        'x_prompt': nrm(ks[0], (BATCH, SEQ, D_MODEL), 1.0),
        'x_sample': nrm(ks[1], (DEC_BATCH, DEC_SEQ, D_MODEL), 1.0),
        'cache_k': nrm(ks[2], (DEPTH, n_pool, PAGE_SIZE, N_HEADS, ATTN_E), 1.0),
        'cache_v': nrm(ks[3], (DEPTH, n_pool, PAGE_SIZE, N_HEADS, ATTN_E), 1.0),
        'state_conv': nrm(ks[4], (DEPTH, DEC_BATCH, CONV_WIDTH - 1, CONV_DIM), 0.5),
        'page_table': perm.reshape(DEC_BATCH, n_pages).astype(jnp.int32),
        'p_prompt': nrm(ks[6], (DEPTH, BATCH, SEQ, PLE_DIM), 1.0),
        'p_sample': nrm(ks[7], (DEPTH, DEC_BATCH, DEC_SEQ, PLE_DIM), 1.0),
        'g_pre_mix': gain(ks[8], (DEPTH, D_MODEL)),
        'w_in': nrm(ks[9], (DEPTH, D_MODEL, IN_COLS), D_MODEL ** -0.5),
        'lambda_q1': nrm(ks[10], (DEPTH, HEAD_DIM), 0.1),
        'lambda_k1': nrm(ks[11], (DEPTH, HEAD_DIM), 0.1),
        'lambda_q2': nrm(ks[12], (DEPTH, HEAD_DIM), 0.1),
        'lambda_k2': nrm(ks[13], (DEPTH, HEAD_DIM), 0.1),
        'g_subln': gain(ks[14], (DEPTH, ATTN_E)),
        'w_attn_out': nrm(ks[15], (DEPTH, ATTN_WIDTH, D_MODEL), ATTN_WIDTH ** -0.5),
        'conv_w': nrm(ks[16], (DEPTH, CONV_WIDTH, CONV_DIM), CONV_WIDTH ** -0.5),
        'conv_b': nrm(ks[17], (DEPTH, CONV_DIM), 0.02),
        'g_conv_norm': gain(ks[18], (DEPTH, CONV_DIM)),
        'b_conv_norm': nrm(ks[19], (DEPTH, CONV_DIM), 0.02),
        'w_conv_out': nrm(ks[20], (DEPTH, CONV_DIM, D_MODEL), CONV_DIM ** -0.5),
        'w_out': nrm(ks[21], (DEPTH, D_MODEL, D_MODEL), D_MODEL ** -0.5),
        'g_post_mix': gain(ks[22], (DEPTH, D_MODEL)),
        'g_pre_ffn': gain(ks[23], (DEPTH, D_MODEL)),
        'w_ffn_gate': nrm(ks[24], (DEPTH, D_MODEL, D_FF), D_MODEL ** -0.5),
        'w_ffn_up': nrm(ks[25], (DEPTH, D_MODEL, D_FF), D_MODEL ** -0.5),
        'w_ffn_down': nrm(ks[26], (DEPTH, D_FF, D_MODEL), D_FF ** -0.5),
        'g_post_ffn': gain(ks[27], (DEPTH, D_MODEL)),
        'w_ple_proj': nrm(ks[28], (DEPTH, PLE_DIM, D_MODEL), PLE_DIM ** -0.5),
        'w_ple_gate': nrm(ks[29], (DEPTH, D_MODEL, D_MODEL), D_MODEL ** -0.5),
    }


def reference(x_prompt, x_sample, cache_k, cache_v, state_conv, page_table, p_prompt, p_sample,
              g_pre_mix, w_in, lambda_q1, lambda_k1, lambda_q2, lambda_k2, g_subln, w_attn_out,
              conv_w, conv_b, g_conv_norm, b_conv_norm, w_conv_out, w_out, g_post_mix,
              g_pre_ffn, w_ffn_gate, w_ffn_up, w_ffn_down, g_post_ffn, w_ple_proj, w_ple_gate):
    slopes = _alibi_slopes()
    dec_b, n_pages = page_table.shape
    page_size = cache_k.shape[2]
    yp, ys = x_prompt, x_sample
    kp_l, vp_l, cp_l, ks_l, vs_l, cs_l = [], [], [], [], [], []
    for i in range(DEPTH):
        lam_init = 0.8 - 0.6 * math.exp(-0.3 * i)
        lw = {
            'g_pre_mix': g_pre_mix[i], 'w_in': w_in[i],
            'lambda_q1': lambda_q1[i], 'lambda_k1': lambda_k1[i],
            'lambda_q2': lambda_q2[i], 'lambda_k2': lambda_k2[i],
            'g_subln': g_subln[i], 'w_attn_out': w_attn_out[i],
            'conv_w': conv_w[i], 'conv_b': conv_b[i],
            'g_conv_norm': g_conv_norm[i], 'b_conv_norm': b_conv_norm[i],
            'w_conv_out': w_conv_out[i], 'w_out': w_out[i], 'g_post_mix': g_post_mix[i],
            'g_pre_ffn': g_pre_ffn[i], 'w_ffn_gate': w_ffn_gate[i], 'w_ffn_up': w_ffn_up[i],
            'w_ffn_down': w_ffn_down[i], 'g_post_ffn': g_post_ffn[i],
            'w_ple_proj': w_ple_proj[i], 'w_ple_gate': w_ple_gate[i],
        }
        hist0 = jnp.zeros((yp.shape[0], CONV_WIDTH - 1, CONV_DIM), yp.dtype)
        attend_p = lambda q, k, v, lam: _prompt_attend(q, k, v, lam, slopes)
        yp, kp, vp, cp = _layer(yp, p_prompt[i], hist0, attend_p, lam_init, lw)
        k_past = cache_k[i][page_table].reshape(dec_b, n_pages * page_size, N_HEADS, ATTN_E)
        v_past = cache_v[i][page_table].reshape(dec_b, n_pages * page_size, N_HEADS, ATTN_E)
        attend_s = lambda q, k, v, lam, kpa=k_past, vpa=v_past: _sample_attend(q, k, v, lam, slopes, kpa, vpa)
        ys, ksn, vsn, csn = _layer(ys, p_sample[i], state_conv[i], attend_s, lam_init, lw)
        kp_l.append(kp); vp_l.append(vp); cp_l.append(cp)
        ks_l.append(ksn); vs_l.append(vsn); cs_l.append(csn)
    return (yp, ys, jnp.stack(kp_l), jnp.stack(vp_l), jnp.stack(cp_l),
            jnp.stack(ks_l), jnp.stack(vs_l), jnp.stack(cs_l))
```

```python
import functools
import math

import jax
import jax.numpy as jnp
from jax import lax
from jax.experimental import pallas as pl
from jax.experimental.pallas import tpu as pltpu

F32 = jnp.float32
BF16 = jnp.bfloat16

EPS = 1e-6
NEG_INF = -1e30

LANES = 128
SUBLANES = 8
VMEM_BYTES_V7X = 64 * 1024 * 1024
VMEM_CEILING = VMEM_BYTES_V7X - 8 * 1024 * 1024
VMEM_INTERNAL = 12 * 1024 * 1024

ROW_TILE = 512
COL_TILE = 512
ATTN_TILE = 256
CONV_ROWS = 128


def _tile(n, pref):
    if n <= pref:
        return n
    if pref < LANES:
        assert n % pref == 0, (n, pref)
        return pref
    t = (pref // LANES) * LANES
    while t > LANES and n % t:
        t -= LANES
    assert n % t == 0, (n, pref)
    return t


def _nbytes(shape, dtype):
    return math.prod(shape) * jnp.dtype(dtype).itemsize


def _params(semantics, blocks, scratch=()):
    need = 2 * sum(_nbytes(s, d) for s, d in blocks)
    need += sum(_nbytes(s, d) for s, d in scratch) + VMEM_INTERNAL
    return pltpu.CompilerParams(dimension_semantics=semantics,
                                vmem_limit_bytes=min(need, VMEM_CEILING))


def _dot(a, b):
    return jnp.dot(a, b, preferred_element_type=F32)


def _dot_nt(a, b):
    return lax.dot_general(a, b, (((1,), (1,)), ((), ())), preferred_element_type=F32)


def _rms(x, g):
    return x * lax.rsqrt(jnp.mean(x * x, axis=-1, keepdims=True) + EPS) * g


def _lambda(lq1, lk1, lq2, lk2, lam_init):
    s1 = jnp.sum(lq1[...] * lk1[...], axis=-1, keepdims=True)
    s2 = jnp.sum(lq2[...] * lk2[...], axis=-1, keepdims=True)
    return jnp.exp(s1) - jnp.exp(s2) + lam_init


def _norm_q_kernel(x_ref, g_ref, w_ref, h_ref, q_ref, *, scale):
    @pl.when(pl.program_id(1) == 0)
    def _():
        h_ref[...] = _rms(x_ref[...], g_ref[...]).astype(BF16)

    q_ref[...] = (_dot(h_ref[...], w_ref[...]) * scale).astype(BF16)


def _proj_kernel(h_ref, w_ref, f_ref, b_ref):
    y = _dot(h_ref[...], w_ref[...])
    f_ref[...] = y
    b_ref[...] = y.astype(BF16)


def _glu_kernel(h_ref, wa_ref, wb_ref, u_ref):
    h = h_ref[...]
    u_ref[...] = _dot(h, wa_ref[...]) * jax.nn.sigmoid(_dot(h, wb_ref[...]))


def _merge_kernel(h_ref, o_ref, c_ref, wga_ref, wgc_ref, wa_ref, wc_ref, m_ref):
    h = h_ref[...]
    a = _dot(o_ref[...], wa_ref[...])
    c = _dot(c_ref[...], wc_ref[...])
    m = jax.nn.sigmoid(_dot(h, wga_ref[...])) * a + jax.nn.sigmoid(_dot(h, wgc_ref[...])) * c
    m_ref[...] = m.astype(BF16)


def _out_kernel(m_ref, w_ref, x_ref, gpost_ref, gpre_ref, x1_ref, f_ref):
    mix = _dot(m_ref[...], w_ref[...])
    x1 = x_ref[...] + _rms(mix, gpost_ref[...])
    x1_ref[...] = x1
    f_ref[...] = _rms(x1, gpre_ref[...]).astype(BF16)


def _ffn_up_kernel(f_ref, wg_ref, wu_ref, hd_ref):
    f = f_ref[...]
    hd_ref[...] = (jax.nn.silu(_dot(f, wg_ref[...])) * _dot(f, wu_ref[...])).astype(BF16)


def _ffn_down_kernel(hd_ref, w_ref, x1_ref, g_ref, x2_ref, acc_ref):
    k = pl.program_id(1)

    @pl.when(k == 0)
    def _():
        acc_ref[...] = jnp.zeros_like(acc_ref)

    acc_ref[...] += _dot(hd_ref[...], w_ref[...])

    @pl.when(k == pl.num_programs(1) - 1)
    def _():
        x2_ref[...] = x1_ref[...] + _rms(acc_ref[...], g_ref[...])


def _ple_kernel(x2_ref, pe_ref, wg_ref, wp_ref, y_ref, x2b_ref):
    j = pl.program_id(1)
    tn = y_ref.shape[1]

    @pl.when(j == 0)
    def _():
        x2b_ref[...] = x2_ref[...].astype(BF16)

    gate = jax.nn.sigmoid(_dot(x2b_ref[...], wg_ref[...]))
    res = x2_ref[:, pl.ds(pl.multiple_of(j * tn, tn), tn)]
    y_ref[...] = res + gate * _dot(pe_ref[...].astype(BF16), wp_ref[...])


def _prompt_attn_kernel(slopes_ref, lq1, lk1, lq2, lk2, gsub_ref, q_ref, k_ref, v_ref,
                        o_ref, bias_ref, *, lam_init, dh):
    t = q_ref.shape[1]
    head = pl.program_id(1)
    qi = pl.program_id(2)
    slope = slopes_ref[head]

    @pl.when(qi == 0)
    def _():
        row = lax.broadcasted_iota(jnp.int32, (2 * t, t), 0)
        col = lax.broadcasted_iota(jnp.int32, (2 * t, t), 1)
        rel = jnp.where(row >= t, row - t, row) - col
        lin = -slope * rel.astype(F32)
        bias_ref[0] = lin
        bias_ref[1] = jnp.where(rel >= 0, lin, NEG_INF)

    q = q_ref[0]
    lane = lax.broadcasted_iota(jnp.int32, q.shape, 1)
    zero = jnp.zeros_like(q)
    qs = jnp.concatenate([jnp.where(lane < dh, q, zero), jnp.where(lane >= dh, q, zero)], axis=0)

    def block(kj, carry, bias, shift):
        m, l, acc = carry
        start = pl.multiple_of(kj * t, t)
        kb = k_ref[0, pl.ds(start, t), :]
        vb = v_ref[0, pl.ds(start, t), :]
        s = _dot_nt(qs, kb) + bias
        m_new = jnp.maximum(m, jnp.max(s, axis=-1, keepdims=True) + shift)
        p = jnp.exp(s - (m_new - shift))
        alpha = jnp.exp(m - m_new)
        l = alpha * l + jnp.sum(p, axis=-1, keepdims=True)
        acc = alpha * acc + _dot(p.astype(BF16), vb)
        return m_new, l, acc

    def past_block(kj, carry):
        shift = -slope * ((qi - kj) * t).astype(F32)
        return block(kj, carry, bias_ref[0], shift)

    init = (jnp.full((2 * t, 1), NEG_INF, F32), jnp.zeros((2 * t, 1), F32),
            jnp.zeros((2 * t, 2 * dh), F32))
    carry = lax.fori_loop(0, qi, past_block, init)
    _, l, acc = block(qi, carry, bias_ref[1], 0.0)

    lam = _lambda(lq1, lk1, lq2, lk2, lam_init)
    o = acc[:t] / l[:t] - lam * (acc[t:] / l[t:])
    o_ref[0] = (_rms(o, gsub_ref[...]) * (1.0 - lam_init)).astype(BF16)


def _decode_attn_kernel(pt_ref, slope_ref, khead_ref, kpos_ref, lq1, lk1, lq2, lk2, gsub_ref,
                        q_ref, kn_ref, vn_ref, kc_ref, vc_ref, o_ref, qs_ref, bias_ref, m_ref, l_ref,
                        acc_ref, *, lam_init, dh):
    del pt_ref
    page = pl.program_id(1)
    n_pages = pl.num_programs(1)
    n_heads = q_ref.shape[1]
    ps = kc_ref.shape[1] // n_heads
    slope = slope_ref[...]

    @pl.when(page == 0)
    def _():
        q = q_ref[0]
        lane = lax.broadcasted_iota(jnp.int32, q.shape, 1)
        zero = jnp.zeros_like(q)
        qs_ref[...] = jnp.concatenate([jnp.where(lane < dh, q, zero), jnp.where(lane >= dh, q, zero)], axis=0)
        row_head = lax.broadcasted_iota(jnp.int32, bias_ref.shape, 0) % n_heads
        bias_ref[...] = jnp.where(khead_ref[...] == row_head, slope * kpos_ref[...], NEG_INF)
        m_ref[...] = jnp.full_like(m_ref, NEG_INF)
        l_ref[...] = jnp.zeros_like(l_ref)
        acc_ref[...] = jnp.zeros_like(acc_ref)

    shift = -slope * ((n_pages - page) * ps).astype(F32)
    s = _dot_nt(qs_ref[...], kc_ref[0].astype(BF16)) + bias_ref[...]
    m_old = m_ref[...]
    m_new = jnp.maximum(m_old, jnp.max(s, axis=-1, keepdims=True) + shift)
    p = jnp.exp(s - (m_new - shift))
    alpha = jnp.exp(m_old - m_new)
    m_ref[...] = m_new
    l_ref[...] = alpha * l_ref[...] + jnp.sum(p, axis=-1, keepdims=True)
    acc_ref[...] = alpha * acc_ref[...] + _dot(p.astype(BF16), vc_ref[0].astype(BF16))

    @pl.when(page == n_pages - 1)
    def _():
        prod = q_ref[0].astype(F32) * kn_ref[0].astype(F32)
        lane = lax.broadcasted_iota(jnp.int32, prod.shape, 1)
        s1 = jnp.sum(jnp.where(lane < dh, prod, 0.0), axis=-1, keepdims=True)
        s2 = jnp.sum(jnp.where(lane >= dh, prod, 0.0), axis=-1, keepdims=True)
        s_self = jnp.concatenate([s1, s2], axis=0)
        vn = vn_ref[0].astype(F32)
        m_old = m_ref[...]
        m_fin = jnp.maximum(m_old, s_self)
        alpha = jnp.exp(m_old - m_fin)
        p_self = jnp.exp(s_self - m_fin)
        l = alpha * l_ref[...] + p_self
        acc = alpha * acc_ref[...] + p_self * jnp.concatenate([vn, vn], axis=0)
        a = acc / l
        lam = _lambda(lq1, lk1, lq2, lk2, lam_init)
        o = a[:n_heads] - lam * a[n_heads:]
        o_ref[0] = (_rms(o, gsub_ref[...]) * (1.0 - lam_init)).astype(BF16)


def _ln_silu(c, g, b):
    mu = jnp.mean(c, axis=-1, keepdims=True)
    var = jnp.mean(jnp.square(c - mu), axis=-1, keepdims=True)
    y = (c - mu) * lax.rsqrt(var + EPS) * g + b
    return jax.nn.silu(y)


def _prompt_conv_kernel(prev_ref, cur_ref, w_ref, cb_ref, g_ref, b_ref, o_ref, full_ref, c_ref):
    hist = prev_ref.shape[1]
    tt, ch = c_ref.shape
    width = w_ref.shape[0]

    @pl.when(pl.program_id(1) == 0)
    def _():
        full_ref[pl.ds(0, hist), :] = jnp.zeros((hist, ch), F32)

    @pl.when(pl.program_id(1) > 0)
    def _():
        full_ref[pl.ds(0, hist), :] = prev_ref[0]

    full_ref[pl.ds(hist, tt), :] = cur_ref[0]

    def chunk(cc, _):
        lanes = pl.ds(pl.multiple_of(cc * LANES, LANES), LANES)
        acc = jnp.broadcast_to(cb_ref[:, lanes], (tt, LANES))
        for k in range(width):
            acc = acc + w_ref[pl.ds(k, 1), lanes] * full_ref[pl.ds(hist - (width - 1) + k, tt), lanes]
        c_ref[:, lanes] = acc
        return 0

    lax.fori_loop(0, ch // LANES, chunk, 0)
    o_ref[0] = _ln_silu(c_ref[...], g_ref[...], b_ref[...]).astype(BF16)


def _sample_conv_kernel(state_ref, u_ref, w_ref, cb_ref, g_ref, b_ref, o_ref, ns_ref):
    hist = state_ref.shape[0]
    u = u_ref[...]
    c = u * w_ref[pl.ds(hist, 1), :] + cb_ref[...]
    for k in range(hist):
        c = c + state_ref[k] * w_ref[pl.ds(k, 1), :]
    o_ref[...] = _ln_silu(c, g_ref[...], b_ref[...]).astype(BF16)
    for k in range(hist - 1):
        ns_ref[k] = state_ref[k + 1]
    ns_ref[hist - 1] = u


def _row_spec(tm, n):
    return pl.BlockSpec((tm, n), lambda i, j: (i, 0))


def _col_spec(k, tn, off_blocks=0):
    return pl.BlockSpec((k, tn), lambda i, j: (0, j + off_blocks))


def _tile_spec(tm, tn):
    return pl.BlockSpec((tm, tn), lambda i, j: (i, j))


def _vec_spec(n):
    return pl.BlockSpec((1, n), lambda i, j: (0, 0))


def _norm_q(x, g, w_in, n, scale):
    m, d = x.shape
    tm, tn = _tile(m, ROW_TILE), _tile(n, COL_TILE)
    blocks = [((tm, d), F32), ((1, d), F32), ((d, tn), BF16), ((tm, d), BF16), ((tm, tn), BF16)]
    return pl.pallas_call(
        functools.partial(_norm_q_kernel, scale=scale),
        grid=(m // tm, n // tn),
        in_specs=[_row_spec(tm, d), _vec_spec(d), _col_spec(d, tn)],
        out_specs=[_row_spec(tm, d), _tile_spec(tm, tn)],
        out_shape=[jax.ShapeDtypeStruct((m, d), BF16), jax.ShapeDtypeStruct((m, n), BF16)],
        compiler_params=_params(("parallel", "arbitrary"), blocks),
        name="norm_q",
    )(x, g, w_in)


def _proj(h, w_in, off, n):
    m, d = h.shape
    tm, tn = _tile(m, ROW_TILE), _tile(n, COL_TILE)
    assert off % tn == 0
    blocks = [((tm, d), BF16), ((d, tn), BF16), ((tm, tn), F32), ((tm, tn), BF16)]
    return pl.pallas_call(
        _proj_kernel,
        grid=(m // tm, n // tn),
        in_specs=[_row_spec(tm, d), _col_spec(d, tn, off // tn)],
        out_specs=[_tile_spec(tm, tn), _tile_spec(tm, tn)],
        out_shape=[jax.ShapeDtypeStruct((m, n), F32), jax.ShapeDtypeStruct((m, n), BF16)],
        compiler_params=_params(("parallel", "arbitrary"), blocks),
        name="proj_kv",
    )(h, w_in)


def _glu(h, w_in, off, n):
    m, d = h.shape
    tm, tn = _tile(m, ROW_TILE), _tile(n, COL_TILE)
    assert off % tn == 0 and n % tn == 0
    blocks = [((tm, d), BF16), ((d, tn), BF16), ((d, tn), BF16), ((tm, tn), F32)]
    return pl.pallas_call(
        _glu_kernel,
        grid=(m // tm, n // tn),
        in_specs=[_row_spec(tm, d), _col_spec(d, tn, off // tn), _col_spec(d, tn, (off + n) // tn)],
        out_specs=_tile_spec(tm, tn),
        out_shape=jax.ShapeDtypeStruct((m, n), F32),
        compiler_params=_params(("parallel", "arbitrary"), blocks),
        name="conv_glu",
    )(h, w_in, w_in)


def _merge(h, o, c, w_in, off, w_attn_out, w_conv_out):
    m, d = h.shape
    aw, cd = o.shape[1], c.shape[1]
    tm, tn = _tile(m, ROW_TILE), _tile(d, COL_TILE // 2)
    assert off % tn == 0
    blocks = [((tm, d), BF16), ((tm, aw), BF16), ((tm, cd), BF16), ((d, tn), BF16), ((d, tn), BF16),
              ((aw, tn), BF16), ((cd, tn), BF16), ((tm, tn), BF16)]
    return pl.pallas_call(
        _merge_kernel,
        grid=(m // tm, d // tn),
        in_specs=[_row_spec(tm, d), _row_spec(tm, aw), _row_spec(tm, cd),
                  _col_spec(d, tn, off // tn), _col_spec(d, tn, (off + d) // tn),
                  _col_spec(aw, tn), _col_spec(cd, tn)],
        out_specs=_tile_spec(tm, tn),
        out_shape=jax.ShapeDtypeStruct((m, d), BF16),
        compiler_params=_params(("parallel", "arbitrary"), blocks),
        name="gated_merge",
    )(h, o, c, w_in, w_in, w_attn_out, w_conv_out)


def _out_proj(mix_in, w_out, x, g_post, g_pre):
    m, d = x.shape
    tm = _tile(m, ROW_TILE // 2)
    row = pl.BlockSpec((tm, d), lambda i: (i, 0))
    vec = pl.BlockSpec((1, d), lambda i: (0, 0))
    blocks = [((tm, d), BF16), ((d, d), BF16), ((tm, d), F32), ((tm, d), F32), ((tm, d), BF16)]
    return pl.pallas_call(
        _out_kernel,
        grid=(m // tm,),
        in_specs=[row, pl.BlockSpec((d, d), lambda i: (0, 0)), row, vec, vec],
        out_specs=[row, row],
        out_shape=[jax.ShapeDtypeStruct((m, d), F32), jax.ShapeDtypeStruct((m, d), BF16)],
        compiler_params=_params(("parallel",), blocks),
        name="out_proj",
    )(mix_in, w_out, x, g_post, g_pre)


def _ffn_up(f, w_gate, w_up):
    m, d = f.shape
    dff = w_gate.shape[1]
    tm, tn = _tile(m, ROW_TILE), _tile(dff, COL_TILE)
    blocks = [((tm, d), BF16), ((d, tn), BF16), ((d, tn), BF16), ((tm, tn), BF16)]
    return pl.pallas_call(
        _ffn_up_kernel,
        grid=(m // tm, dff // tn),
        in_specs=[_row_spec(tm, d), _col_spec(d, tn), _col_spec(d, tn)],
        out_specs=_tile_spec(tm, tn),
        out_shape=jax.ShapeDtypeStruct((m, dff), BF16),
        compiler_params=_params(("parallel", "arbitrary"), blocks),
        name="ffn_up",
    )(f, w_gate, w_up)


def _ffn_down(hd, w_down, x1, g_post):
    m, dff = hd.shape
    d = x1.shape[1]
    tm, tk = _tile(m, ROW_TILE), _tile(dff, COL_TILE)
    row = pl.BlockSpec((tm, d), lambda i, k: (i, 0))
    blocks = [((tm, tk), BF16), ((tk, d), BF16), ((tm, d), F32), ((tm, d), F32)]
    scratch = [((tm, d), F32)]
    return pl.pallas_call(
        _ffn_down_kernel,
        grid=(m // tm, dff // tk),
        in_specs=[pl.BlockSpec((tm, tk), lambda i, k: (i, k)), pl.BlockSpec((tk, d), lambda i, k: (k, 0)),
                  row, pl.BlockSpec((1, d), lambda i, k: (0, 0))],
        out_specs=row,
        out_shape=jax.ShapeDtypeStruct((m, d), F32),
        scratch_shapes=[pltpu.VMEM(s, dt) for s, dt in scratch],
        compiler_params=_params(("parallel", "arbitrary"), blocks, scratch),
        name="ffn_down",
    )(hd, w_down, x1, g_post)


def _ple(x2, pe, w_gate, w_proj):
    m, d = x2.shape
    pd = pe.shape[1]
    tm, tn = _tile(m, ROW_TILE), _tile(d, COL_TILE)
    blocks = [((tm, d), F32), ((tm, pd), F32), ((d, tn), BF16), ((pd, tn), BF16), ((tm, tn), F32)]
    scratch = [((tm, d), BF16)]
    return pl.pallas_call(
        _ple_kernel,
        grid=(m // tm, d // tn),
        in_specs=[_row_spec(tm, d), _row_spec(tm, pd), _col_spec(d, tn), _col_spec(pd, tn)],
        out_specs=_tile_spec(tm, tn),
        out_shape=jax.ShapeDtypeStruct((m, d), F32),
        scratch_shapes=[pltpu.VMEM(s, dt) for s, dt in scratch],
        compiler_params=_params(("parallel", "arbitrary"), blocks, scratch),
        name="ple",
    )(x2, pe, w_gate, w_proj)


def _lambda_specs(dh, index_map):
    return [pl.BlockSpec((1, dh), index_map)] * 4


def _prompt_attention(q, k, v, slopes, lams, g_subln, lam_init, n_heads):
    b, t, aw = q.shape
    e = aw // n_heads
    dh = e // 2
    tq = _tile(t, ATTN_TILE)
    const = lambda bi, h, qi: (0, 0)
    seq = pl.BlockSpec((1, t, e), lambda bi, h, qi: (bi, 0, h))
    blk = pl.BlockSpec((1, tq, e), lambda bi, h, qi: (bi, qi, h))
    blocks = [((t, e), BF16), ((t, e), BF16), ((tq, e), BF16), ((tq, e), BF16)]
    scratch = [((2, 2 * tq, tq), F32)]
    return pl.pallas_call(
        functools.partial(_prompt_attn_kernel, lam_init=lam_init, dh=dh),
        grid=(b, n_heads, t // tq),
        in_specs=[pl.BlockSpec(memory_space=pltpu.SMEM)] + _lambda_specs(dh, const)
        + [pl.BlockSpec((1, e), const), blk, seq, seq],
        out_specs=blk,
        out_shape=jax.ShapeDtypeStruct((b, t, aw), BF16),
        scratch_shapes=[pltpu.VMEM(s, d) for s, d in scratch],
        compiler_params=_params(("parallel", "parallel", "arbitrary"), blocks, scratch),
        name="prompt_attention",
    )(slopes, *lams, g_subln, q, k, v)


def _decode_attention(q, k_new, v_new, cache_k, cache_v, page_table, slopes, lams, g_subln,
                      lam_init, n_heads):
    bs, aw = q.shape
    n_pool, ps = cache_k.shape[:2]
    n_pages = page_table.shape[1]
    e = aw // n_heads
    dh = e // 2
    rows, cols = 2 * n_heads, ps * n_heads
    const = lambda bi, p, pt: (0, 0)
    tok = pl.BlockSpec((1, n_heads, e), lambda bi, p, pt: (bi, 0, 0))
    page = pl.BlockSpec((1, cols, e), lambda bi, p, pt: (pt[bi * n_pages + p], 0, 0))
    col = jnp.arange(cols, dtype=jnp.int32)[None, :]
    blocks = [((cols, e), F32), ((cols, e), F32)]
    scratch = [((rows, e), BF16), ((rows, cols), F32), ((rows, 1), F32), ((rows, 1), F32), ((rows, e), F32)]
    grid_spec = pltpu.PrefetchScalarGridSpec(
        num_scalar_prefetch=1,
        grid=(bs, n_pages),
        in_specs=[pl.BlockSpec((rows, 1), const), pl.BlockSpec((1, cols), const), pl.BlockSpec((1, cols), const)]
        + _lambda_specs(dh, const) + [pl.BlockSpec((1, e), const), tok, tok, tok, page, page],
        out_specs=tok,
        scratch_shapes=[pltpu.VMEM(s, d) for s, d in scratch],
    )
    o = pl.pallas_call(
        functools.partial(_decode_attn_kernel, lam_init=lam_init, dh=dh),
        grid_spec=grid_spec,
        out_shape=jax.ShapeDtypeStruct((bs, n_heads, e), BF16),
        compiler_params=_params(("parallel", "arbitrary"), blocks, scratch),
        name="decode_attention",
    )(page_table.reshape(-1), jnp.tile(slopes, 2)[:, None], col % n_heads, (col // n_heads).astype(F32),
      *lams, g_subln, q.reshape(bs, n_heads, e), k_new.reshape(bs, n_heads, e), v_new.reshape(bs, n_heads, e),
      cache_k.reshape(n_pool, cols, e), cache_v.reshape(n_pool, cols, e))
    return o.reshape(bs, aw)


def _prompt_conv(u, conv_w, conv_b, g, b):
    bsz, t, ch = u.shape
    width = conv_w.shape[0]
    hist = -(-(width - 1) // SUBLANES) * SUBLANES
    tt = _tile(t, CONV_ROWS)
    assert tt % hist == 0
    const = lambda bi, ti: (0, 0)
    vec = pl.BlockSpec((1, ch), const)
    blocks = [((hist, ch), F32), ((tt, ch), F32), ((width, ch), F32), ((tt, ch), BF16)]
    scratch = [((hist + tt, ch), F32), ((tt, ch), F32)]
    return pl.pallas_call(
        _prompt_conv_kernel,
        grid=(bsz, t // tt),
        in_specs=[pl.BlockSpec((1, hist, ch), lambda bi, ti: (bi, jnp.maximum(ti * (tt // hist) - 1, 0), 0)),
                  pl.BlockSpec((1, tt, ch), lambda bi, ti: (bi, ti, 0)),
                  pl.BlockSpec((width, ch), const), vec, vec, vec],
        out_specs=pl.BlockSpec((1, tt, ch), lambda bi, ti: (bi, ti, 0)),
        out_shape=jax.ShapeDtypeStruct((bsz, t, ch), BF16),
        scratch_shapes=[pltpu.VMEM(s, d) for s, d in scratch],
        compiler_params=_params(("parallel", "arbitrary"), blocks, scratch),
        name="prompt_conv",
    )(u, u, conv_w, conv_b, g, b)


def _sample_conv(state, u, conv_w, conv_b, g, b):
    hist, bs, ch = state.shape
    width = conv_w.shape[0]
    assert hist == width - 1
    bt = _tile(bs, 4 * SUBLANES) if bs % SUBLANES == 0 else bs
    vec = pl.BlockSpec((1, ch), lambda i: (0, 0))
    st = pl.BlockSpec((hist, bt, ch), lambda i: (0, i, 0))
    blocks = [((hist, bt, ch), F32), ((bt, ch), F32), ((width, ch), F32), ((bt, ch), BF16), ((hist, bt, ch), F32)]
    return pl.pallas_call(
        _sample_conv_kernel,
        grid=(bs // bt,),
        in_specs=[st, pl.BlockSpec((bt, ch), lambda i: (i, 0)), pl.BlockSpec((width, ch), lambda i: (0, 0)),
                  vec, vec, vec],
        out_specs=[pl.BlockSpec((bt, ch), lambda i: (i, 0)), st],
        out_shape=[jax.ShapeDtypeStruct((bs, ch), BF16), jax.ShapeDtypeStruct((hist, bs, ch), F32)],
        compiler_params=_params(("parallel",), blocks),
        name="sample_conv",
    )(state, u, conv_w, conv_b, g, b)


def _layer(x, pe, attend, conv_branch, lw, dims):
    aw, cd, d = dims["attn_width"], dims["conv_dim"], x.shape[1]
    w_in = lw["w_in"]
    h, q = _norm_q(x, lw["g_pre_mix"], w_in, aw, dims["attn_scale"])
    k_f, k_b = _proj(h, w_in, aw, aw)
    v_f, v_b = _proj(h, w_in, 2 * aw, aw)
    u = _glu(h, w_in, 3 * aw, cd)
    o = attend(q, k_b, v_b)
    c, conv_state = conv_branch(u)
    mix_in = _merge(h, o, c, w_in, 3 * aw + 2 * cd, lw["w_attn_out"], lw["w_conv_out"])
    x1, f = _out_proj(mix_in, lw["w_out"], x, lw["g_post_mix"], lw["g_pre_ffn"])
    hd = _ffn_up(f, lw["w_ffn_gate"], lw["w_ffn_up"])
    x2 = _ffn_down(hd, lw["w_ffn_down"], x1, lw["g_post_ffn"])
    y = _ple(x2, pe, lw["w_ple_gate"], lw["w_ple_proj"])
    return y, k_f, v_f, conv_state


_MATRICES = ("w_in", "w_attn_out", "w_conv_out", "w_out", "w_ffn_gate", "w_ffn_up", "w_ffn_down",
             "w_ple_proj", "w_ple_gate")
_VECTORS = ("g_pre_mix", "g_subln", "conv_b", "g_conv_norm", "b_conv_norm", "g_post_mix", "g_pre_ffn",
            "g_post_ffn", "lambda_q1", "lambda_k1", "lambda_q2", "lambda_k2")


def kernel(x_prompt, x_sample, cache_k, cache_v, state_conv, page_table, p_prompt, p_sample, g_pre_mix, w_in, lambda_q1, lambda_k1, lambda_q2, lambda_k2, g_subln, w_attn_out, conv_w, conv_b, g_conv_norm, b_conv_norm, w_conv_out, w_out, g_post_mix, g_pre_ffn, w_ffn_gate, w_ffn_up, w_ffn_down, g_post_ffn, w_ple_proj, w_ple_gate):
    weights = dict(g_pre_mix=g_pre_mix, w_in=w_in, lambda_q1=lambda_q1, lambda_k1=lambda_k1,
                   lambda_q2=lambda_q2, lambda_k2=lambda_k2, g_subln=g_subln, w_attn_out=w_attn_out,
                   conv_w=conv_w, conv_b=conv_b, g_conv_norm=g_conv_norm, b_conv_norm=b_conv_norm,
                   w_conv_out=w_conv_out, w_out=w_out, g_post_mix=g_post_mix, g_pre_ffn=g_pre_ffn,
                   w_ffn_gate=w_ffn_gate, w_ffn_up=w_ffn_up, w_ffn_down=w_ffn_down, g_post_ffn=g_post_ffn,
                   w_ple_proj=w_ple_proj, w_ple_gate=w_ple_gate)
    depth = w_in.shape[0]
    bp, t, d = x_prompt.shape
    bs, ts, _ = x_sample.shape
    assert ts == 1, "the sample group decodes one token per sequence"
    n_heads, e = cache_k.shape[3], cache_k.shape[4]
    dims = dict(attn_width=n_heads * e, conv_dim=conv_w.shape[2], attn_scale=(e // 2) ** -0.5)
    slopes = 2.0 ** (-8.0 * jnp.arange(1, n_heads + 1, dtype=F32) / n_heads)

    yp = x_prompt.reshape(bp * t, d)
    ys = x_sample.reshape(bs, d)
    n_pool = cache_k.shape[1]
    pool_k = cache_k.reshape((depth * n_pool,) + cache_k.shape[2:])
    pool_v = cache_v.reshape((depth * n_pool,) + cache_v.shape[2:])
    outs = [[] for _ in range(6)]
    for i in range(depth):
        lam_init = 0.8 - 0.6 * math.exp(-0.3 * i)
        lw = {n: weights[n][i].astype(BF16) for n in _MATRICES}
        lw.update({n: weights[n][i][None, :] for n in _VECTORS})
        lw["conv_w"] = conv_w[i]
        lams = [lw[n] for n in ("lambda_q1", "lambda_k1", "lambda_q2", "lambda_k2")]
        ln = (lw["conv_b"], lw["g_conv_norm"], lw["b_conv_norm"])

        def attend_p(q, k, v):
            shape = (bp, t, dims["attn_width"])
            o = _prompt_attention(q.reshape(shape), k.reshape(shape), v.reshape(shape), slopes, lams,
                                  lw["g_subln"], lam_init, n_heads)
            return o.reshape(bp * t, -1)

        def conv_p(u):
            u3 = u.reshape(bp, t, -1)
            c = _prompt_conv(u3, lw["conv_w"], *ln)
            return c.reshape(bp * t, -1), u3[:, t - (conv_w.shape[1] - 1):, :]

        def attend_s(q, k, v):
            return _decode_attention(q, k, v, pool_k, pool_v, page_table + i * n_pool, slopes, lams,
                                     lw["g_subln"], lam_init, n_heads)

        def conv_s(u):
            c, state = _sample_conv(jnp.swapaxes(state_conv[i], 0, 1), u, lw["conv_w"], *ln)
            return c, jnp.swapaxes(state, 0, 1)

        yp, kp, vp, cp = _layer(yp, p_prompt[i].reshape(bp * t, -1), attend_p, conv_p, lw, dims)
        ys, ksn, vsn, csn = _layer(ys, p_sample[i].reshape(bs, -1), attend_s, conv_s, lw, dims)
        for lst, val in zip(outs, (kp.reshape(bp, t, n_heads, e), vp.reshape(bp, t, n_heads, e), cp,
                                   ksn.reshape(bs, 1, n_heads, e), vsn.reshape(bs, 1, n_heads, e), csn)):
            lst.append(val)
    stack = (lambda o: o[0][None]) if depth == 1 else jnp.stack
    return (yp.reshape(bp, t, d), ys.reshape(bs, 1, d), *[stack(o) for o in outs])
```

```python
import functools
import math

import jax
import jax.numpy as jnp
from jax import lax
from jax.experimental import pallas as pl
from jax.experimental.pallas import tpu as pltpu

F32 = jnp.float32
BF16 = jnp.bfloat16

EPS = 1e-6
NEG_INF = -1e30

LANES = 128
SUBLANES = 8
VMEM_BYTES_V7X = 64 * 1024 * 1024
VMEM_CEILING = VMEM_BYTES_V7X - 8 * 1024 * 1024
VMEM_INTERNAL = 12 * 1024 * 1024

ROW_TILE = 1024
COL_TILE = 512
NORM_ROWS = 256
ATTN_TILE = 512
CONV_ROWS = 128
DECODE_PAGES = 4

LOG2E = 1.4426950408889634
BF16_EXACT_INT = 256
BF16_PIECES = 3


def _tile(n, pref):
    if n <= pref:
        return n
    if pref < LANES:
        assert n % pref == 0, (n, pref)
        return pref
    t = (pref // LANES) * LANES
    while t > LANES and n % t:
        t -= LANES
    assert n % t == 0, (n, pref)
    return t


def _nbytes(shape, dtype):
    return math.prod(shape) * jnp.dtype(dtype).itemsize


def _params(semantics, blocks, scratch=()):
    need = 2 * sum(_nbytes(s, d) for s, d in blocks)
    need += sum(_nbytes(s, d) for s, d in scratch) + VMEM_INTERNAL
    return pltpu.CompilerParams(dimension_semantics=semantics,
                                vmem_limit_bytes=min(need, VMEM_CEILING))


def _dot(a, b):
    return jnp.dot(a, b, preferred_element_type=F32)


def _dot_nt(a, b):
    return lax.dot_general(a, b, (((1,), (1,)), ((), ())), preferred_element_type=F32)


def _rms(x, g):
    return x * lax.rsqrt(jnp.mean(x * x, axis=-1, keepdims=True) + EPS) * g


def _lambda(lq1, lk1, lq2, lk2, lam_init):
    s1 = jnp.sum(lq1[...] * lk1[...], axis=-1, keepdims=True)
    s2 = jnp.sum(lq2[...] * lk2[...], axis=-1, keepdims=True)
    return jnp.exp(s1) - jnp.exp(s2) + lam_init


def _norm_q_kernel(x_ref, g_ref, w_ref, h_ref, q_ref, *, scale):
    @pl.when(pl.program_id(1) == 0)
    def _():
        rows = _tile(x_ref.shape[0], NORM_ROWS)
        for r in range(0, x_ref.shape[0], rows):
            h_ref[pl.ds(r, rows), :] = _rms(x_ref[pl.ds(r, rows), :], g_ref[...]).astype(BF16)

    q_ref[...] = (_dot(h_ref[...], w_ref[...]) * scale).astype(BF16)


def _proj_kernel(h_ref, w_ref, y_ref):
    y_ref[...] = _dot(h_ref[...], w_ref[...])


def _glu_kernel(h_ref, wa_ref, wb_ref, u_ref):
    h = h_ref[...]
    u_ref[...] = _dot(h, wa_ref[...]) * jax.nn.sigmoid(_dot(h, wb_ref[...]))


def _merge_kernel(h_ref, o_ref, c_ref, wga_ref, wgc_ref, wa_ref, wc_ref, m_ref):
    h = h_ref[...]
    a = _dot(o_ref[...], wa_ref[...])
    c = _dot(c_ref[...], wc_ref[...])
    m = jax.nn.sigmoid(_dot(h, wga_ref[...])) * a + jax.nn.sigmoid(_dot(h, wgc_ref[...])) * c
    m_ref[...] = m.astype(BF16)


def _out_kernel(m_ref, w_ref, x_ref, gpost_ref, gpre_ref, x1_ref, f_ref):
    mix = _dot(m_ref[...], w_ref[...])
    x1 = x_ref[...] + _rms(mix, gpost_ref[...])
    x1_ref[...] = x1
    f_ref[...] = _rms(x1, gpre_ref[...]).astype(BF16)


def _ffn_up_kernel(f_ref, wg_ref, wu_ref, hd_ref):
    f = f_ref[...]
    hd_ref[...] = (jax.nn.silu(_dot(f, wg_ref[...])) * _dot(f, wu_ref[...])).astype(BF16)


def _ffn_down_kernel(hd_ref, w_ref, x1_ref, g_ref, x2_ref, acc_ref):
    k = pl.program_id(1)

    @pl.when(k == 0)
    def _():
        acc_ref[...] = jnp.zeros_like(acc_ref)

    acc_ref[...] += _dot(hd_ref[...], w_ref[...])

    @pl.when(k == pl.num_programs(1) - 1)
    def _():
        x2_ref[...] = x1_ref[...] + _rms(acc_ref[...], g_ref[...])


def _ple_kernel(x2_ref, pe_ref, wg_ref, wp_ref, y_ref, x2b_ref):
    j = pl.program_id(1)
    tn = y_ref.shape[1]

    @pl.when(j == 0)
    def _():
        x2b_ref[...] = x2_ref[...].astype(BF16)

    gate = jax.nn.sigmoid(_dot(x2b_ref[...], wg_ref[...]))
    res = x2_ref[:, pl.ds(pl.multiple_of(j * tn, tn), tn)]
    y_ref[...] = res + gate * _dot(pe_ref[...].astype(BF16), wp_ref[...])


def _bf16_pieces(x):
    pieces = []
    for _ in range(BF16_PIECES):
        p = x.astype(BF16).astype(F32)
        pieces.append(p)
        x = x - p
    return pieces


def _alibi_lanes(lane0, lane, groups):
    out = jnp.zeros(lane.shape, F32)
    for g, pieces in enumerate(groups):
        for i, p in enumerate(pieces):
            out = jnp.where(lane == lane0 + g * BF16_PIECES + i, p, out)
    return out


def _prompt_attn_kernel(slopes_ref, lq1, lk1, lq2, lk2, gsub_ref, q_ref, k_ref, v_ref, o_ref,
                        k1_ref, k2_ref, vb_ref, mask_ref, *, lam_init, dh):
    t = q_ref.shape[1]
    seq, e = k_ref.shape[1], k_ref.shape[2]
    head = pl.program_id(1)
    qi = pl.program_id(2)
    sigma = slopes_ref[head] * LOG2E

    @pl.when(qi == 0)
    def _():
        lane = lax.broadcasted_iota(jnp.int32, (seq, e), 1)
        pos = lax.broadcasted_iota(jnp.int32, (seq, e), 0)
        lo = pos % BF16_EXACT_INT
        hi, lo, one = (pos - lo).astype(F32), lo.astype(F32), jnp.ones((seq, e), F32)
        groups = [[hi] * BF16_PIECES, [lo] * BF16_PIECES, [one] * BF16_PIECES]
        k = k_ref[0]
        k1_ref[...] = jnp.where(lane < dh, k, _alibi_lanes(dh, lane, groups)).astype(BF16)
        k2_ref[...] = jnp.where(lane >= dh, k, _alibi_lanes(0, lane, groups)).astype(BF16)
        vb_ref[...] = v_ref[0].astype(BF16)
        row = lax.broadcasted_iota(jnp.int32, (2 * t, t), 0)
        col = lax.broadcasted_iota(jnp.int32, (2 * t, t), 1)
        mask_ref[...] = jnp.where(jnp.where(row >= t, row - t, row) >= col, 0.0, NEG_INF)

    q = q_ref[0].astype(F32)
    lane = lax.broadcasted_iota(jnp.int32, (t, e), 1)
    qpos = (qi * t + lax.broadcasted_iota(jnp.int32, (t, e), 0)).astype(F32)
    sig = jnp.full((t, e), sigma, F32)
    sig_p = _bf16_pieces(sig)
    groups = [sig_p, sig_p, _bf16_pieces(-sig * qpos)]
    q1 = jnp.where(lane < dh, q, _alibi_lanes(dh, lane, groups)).astype(BF16)
    q2 = jnp.where(lane >= dh, q, _alibi_lanes(0, lane, groups)).astype(BF16)

    def block(kj, carry, masked):
        m, l, acc = carry
        keys = pl.ds(pl.multiple_of(kj * t, t), t)
        s = jnp.concatenate([_dot_nt(q1, k1_ref[keys, :]), _dot_nt(q2, k2_ref[keys, :])], axis=0)
        if masked:
            s = s + mask_ref[...]
        m_new = jnp.maximum(m, jnp.max(s, axis=-1, keepdims=True))
        p = jnp.exp2(s - m_new)
        alpha = jnp.exp2(m - m_new)
        l = alpha * l + jnp.sum(p, axis=-1, keepdims=True)
        acc = alpha * acc + _dot(p.astype(BF16), vb_ref[keys, :])
        return m_new, l, acc

    init = (jnp.full((2 * t, 1), NEG_INF, F32), jnp.zeros((2 * t, 1), F32), jnp.zeros((2 * t, e), F32))
    carry = lax.fori_loop(0, qi, lambda kj, c: block(kj, c, False), init)
    _, l, acc = block(qi, carry, True)

    lam = _lambda(lq1, lk1, lq2, lk2, lam_init)
    o = acc[:t] / l[:t] - lam * (acc[t:] / l[t:])
    o_ref[0] = (_rms(o, gsub_ref[...]) * (1.0 - lam_init)).astype(BF16)


def _decode_attn_kernel(pt_ref, sig_ref, khead_ref, kpos_ref, lq1, lk1, lq2, lk2, gsub_ref,
                        q_ref, kn_ref, vn_ref, *refs, lam_init, dh, group):
    del pt_ref
    kc_refs, vc_refs = refs[:group], refs[group:2 * group]
    o_ref, qs_ref, bias_ref, m_ref, l_ref, acc_ref = refs[2 * group:]
    step = pl.program_id(1)
    n_steps = pl.num_programs(1)
    n_heads = q_ref.shape[1]
    cols = kc_refs[0].shape[1]
    ps = cols // n_heads
    sigma = sig_ref[...]

    @pl.when(step == 0)
    def _():
        q = q_ref[0]
        lane = lax.broadcasted_iota(jnp.int32, q.shape, 1)
        zero = jnp.zeros_like(q)
        qs_ref[...] = jnp.concatenate([jnp.where(lane < dh, q, zero), jnp.where(lane >= dh, q, zero)], axis=0)
        row_head = lax.broadcasted_iota(jnp.int32, bias_ref.shape, 0) % n_heads
        bias_ref[...] = jnp.where(khead_ref[...] == row_head, sigma * kpos_ref[...], NEG_INF)
        m_ref[...] = jnp.full_like(m_ref, NEG_INF)
        l_ref[...] = jnp.zeros_like(l_ref)
        acc_ref[...] = jnp.zeros_like(acc_ref)

    shift = -sigma * ((n_steps - step) * (group * ps)).astype(F32)
    qs = qs_ref[...]
    s = jnp.concatenate([_dot_nt(qs, kc[0].astype(BF16)) for kc in kc_refs], axis=1) + bias_ref[...]
    m_old = m_ref[...]
    m_new = jnp.maximum(m_old, jnp.max(s, axis=-1, keepdims=True) + shift)
    p = jnp.exp2(s - (m_new - shift)).astype(BF16)
    alpha = jnp.exp2(m_old - m_new)
    m_ref[...] = m_new
    l_ref[...] = alpha * l_ref[...] + jnp.sum(p.astype(F32), axis=-1, keepdims=True)
    pv = _dot(p[:, :cols], vc_refs[0][0].astype(BF16))
    for g in range(1, group):
        pv = pv + _dot(p[:, g * cols:(g + 1) * cols], vc_refs[g][0].astype(BF16))
    acc_ref[...] = alpha * acc_ref[...] + pv

    @pl.when(step == n_steps - 1)
    def _():
        prod = q_ref[0].astype(F32) * kn_ref[0].astype(BF16).astype(F32)
        lane = lax.broadcasted_iota(jnp.int32, prod.shape, 1)
        s1 = jnp.sum(jnp.where(lane < dh, prod, 0.0), axis=-1, keepdims=True)
        s2 = jnp.sum(jnp.where(lane >= dh, prod, 0.0), axis=-1, keepdims=True)
        s_self = jnp.concatenate([s1, s2], axis=0)
        vn = vn_ref[0].astype(BF16).astype(F32)
        m_old = m_ref[...]
        m_fin = jnp.maximum(m_old, s_self)
        alpha = jnp.exp2(m_old - m_fin)
        p_self = jnp.exp2(s_self - m_fin)
        l = alpha * l_ref[...] + p_self
        acc = alpha * acc_ref[...] + p_self * jnp.concatenate([vn, vn], axis=0)
        a = acc / l
        lam = _lambda(lq1, lk1, lq2, lk2, lam_init)
        o = a[:n_heads] - lam * a[n_heads:]
        o_ref[0] = (_rms(o, gsub_ref[...]) * (1.0 - lam_init)).astype(BF16)


def _ln_silu(c, g, b):
    mu = jnp.mean(c, axis=-1, keepdims=True)
    var = jnp.mean(jnp.square(c - mu), axis=-1, keepdims=True)
    y = (c - mu) * lax.rsqrt(var + EPS) * g + b
    return jax.nn.silu(y)


def _prompt_conv_kernel(prev_ref, cur_ref, w_ref, cb_ref, g_ref, b_ref, o_ref, full_ref, c_ref):
    hist = prev_ref.shape[1]
    tt, ch = c_ref.shape
    width = w_ref.shape[0]

    @pl.when(pl.program_id(1) == 0)
    def _():
        full_ref[pl.ds(0, hist), :] = jnp.zeros((hist, ch), F32)

    @pl.when(pl.program_id(1) > 0)
    def _():
        full_ref[pl.ds(0, hist), :] = prev_ref[0]

    full_ref[pl.ds(hist, tt), :] = cur_ref[0]

    def chunk(cc, _):
        lanes = pl.ds(pl.multiple_of(cc * LANES, LANES), LANES)
        acc = jnp.broadcast_to(cb_ref[:, lanes], (tt, LANES))
        for k in range(width):
            acc = acc + w_ref[pl.ds(k, 1), lanes] * full_ref[pl.ds(hist - (width - 1) + k, tt), lanes]
        c_ref[:, lanes] = acc
        return 0

    lax.fori_loop(0, ch // LANES, chunk, 0)
    o_ref[0] = _ln_silu(c_ref[...], g_ref[...], b_ref[...]).astype(BF16)


def _sample_conv_kernel(state_ref, u_ref, w_ref, cb_ref, g_ref, b_ref, o_ref, ns_ref):
    hist = state_ref.shape[0]
    u = u_ref[...]
    c = u * w_ref[pl.ds(hist, 1), :] + cb_ref[...]
    for k in range(hist):
        c = c + state_ref[k] * w_ref[pl.ds(k, 1), :]
    o_ref[...] = _ln_silu(c, g_ref[...], b_ref[...]).astype(BF16)
    for k in range(hist - 1):
        ns_ref[k] = state_ref[k + 1]
    ns_ref[hist - 1] = u


def _row_spec(tm, n):
    return pl.BlockSpec((tm, n), lambda i, j: (i, 0))


def _col_spec(k, tn, off_blocks=0):
    return pl.BlockSpec((k, tn), lambda i, j: (0, j + off_blocks))


def _tile_spec(tm, tn):
    return pl.BlockSpec((tm, tn), lambda i, j: (i, j))


def _vec_spec(n):
    return pl.BlockSpec((1, n), lambda i, j: (0, 0))


def _norm_q(x, g, w_in, n, scale):
    m, d = x.shape
    tm, tn = _tile(m, ROW_TILE), _tile(n, COL_TILE)
    blocks = [((tm, d), F32), ((1, d), F32), ((d, tn), BF16), ((tm, d), BF16), ((tm, tn), BF16)]
    return pl.pallas_call(
        functools.partial(_norm_q_kernel, scale=scale),
        grid=(m // tm, n // tn),
        in_specs=[_row_spec(tm, d), _vec_spec(d), _col_spec(d, tn)],
        out_specs=[_row_spec(tm, d), _tile_spec(tm, tn)],
        out_shape=[jax.ShapeDtypeStruct((m, d), BF16), jax.ShapeDtypeStruct((m, n), BF16)],
        compiler_params=_params(("parallel", "arbitrary"), blocks),
        name="norm_q",
    )(x, g, w_in)


def _proj(h, w_in, off, n):
    m, d = h.shape
    tm, tn = _tile(m, ROW_TILE), _tile(n, COL_TILE)
    assert off % tn == 0
    blocks = [((tm, d), BF16), ((d, tn), BF16), ((tm, tn), F32)]
    return pl.pallas_call(
        _proj_kernel,
        grid=(m // tm, n // tn),
        in_specs=[_row_spec(tm, d), _col_spec(d, tn, off // tn)],
        out_specs=_tile_spec(tm, tn),
        out_shape=jax.ShapeDtypeStruct((m, n), F32),
        compiler_params=_params(("parallel", "arbitrary"), blocks),
        name="proj_kv",
    )(h, w_in)


def _glu(h, w_in, off, n):
    m, d = h.shape
    tm, tn = _tile(m, ROW_TILE), _tile(n, COL_TILE)
    assert off % tn == 0 and n % tn == 0
    blocks = [((tm, d), BF16), ((d, tn), BF16), ((d, tn), BF16), ((tm, tn), F32)]
    return pl.pallas_call(
        _glu_kernel,
        grid=(m // tm, n // tn),
        in_specs=[_row_spec(tm, d), _col_spec(d, tn, off // tn), _col_spec(d, tn, (off + n) // tn)],
        out_specs=_tile_spec(tm, tn),
        out_shape=jax.ShapeDtypeStruct((m, n), F32),
        compiler_params=_params(("parallel", "arbitrary"), blocks),
        name="conv_glu",
    )(h, w_in, w_in)


def _merge(h, o, c, w_in, off, w_attn_out, w_conv_out):
    m, d = h.shape
    aw, cd = o.shape[1], c.shape[1]
    tm, tn = _tile(m, ROW_TILE), _tile(d, COL_TILE // 2)
    assert off % tn == 0
    blocks = [((tm, d), BF16), ((tm, aw), BF16), ((tm, cd), BF16), ((d, tn), BF16), ((d, tn), BF16),
              ((aw, tn), BF16), ((cd, tn), BF16), ((tm, tn), BF16)]
    return pl.pallas_call(
        _merge_kernel,
        grid=(m // tm, d // tn),
        in_specs=[_row_spec(tm, d), _row_spec(tm, aw), _row_spec(tm, cd),
                  _col_spec(d, tn, off // tn), _col_spec(d, tn, (off + d) // tn),
                  _col_spec(aw, tn), _col_spec(cd, tn)],
        out_specs=_tile_spec(tm, tn),
        out_shape=jax.ShapeDtypeStruct((m, d), BF16),
        compiler_params=_params(("parallel", "arbitrary"), blocks),
        name="gated_merge",
    )(h, o, c, w_in, w_in, w_attn_out, w_conv_out)


def _out_proj(mix_in, w_out, x, g_post, g_pre):
    m, d = x.shape
    tm = _tile(m, ROW_TILE // 4)
    row = pl.BlockSpec((tm, d), lambda i: (i, 0))
    vec = pl.BlockSpec((1, d), lambda i: (0, 0))
    blocks = [((tm, d), BF16), ((d, d), BF16), ((tm, d), F32), ((tm, d), F32), ((tm, d), BF16)]
    return pl.pallas_call(
        _out_kernel,
        grid=(m // tm,),
        in_specs=[row, pl.BlockSpec((d, d), lambda i: (0, 0)), row, vec, vec],
        out_specs=[row, row],
        out_shape=[jax.ShapeDtypeStruct((m, d), F32), jax.ShapeDtypeStruct((m, d), BF16)],
        compiler_params=_params(("parallel",), blocks),
        name="out_proj",
    )(mix_in, w_out, x, g_post, g_pre)


def _ffn_up(f, w_gate, w_up):
    m, d = f.shape
    dff = w_gate.shape[1]
    tm, tn = _tile(m, ROW_TILE), _tile(dff, COL_TILE)
    blocks = [((tm, d), BF16), ((d, tn), BF16), ((d, tn), BF16), ((tm, tn), BF16)]
    return pl.pallas_call(
        _ffn_up_kernel,
        grid=(m // tm, dff // tn),
        in_specs=[_row_spec(tm, d), _col_spec(d, tn), _col_spec(d, tn)],
        out_specs=_tile_spec(tm, tn),
        out_shape=jax.ShapeDtypeStruct((m, dff), BF16),
        compiler_params=_params(("parallel", "arbitrary"), blocks),
        name="ffn_up",
    )(f, w_gate, w_up)


def _ffn_down(hd, w_down, x1, g_post):
    m, dff = hd.shape
    d = x1.shape[1]
    tm, tk = _tile(m, ROW_TILE // 2), _tile(dff, COL_TILE)
    row = pl.BlockSpec((tm, d), lambda i, k: (i, 0))
    blocks = [((tm, tk), BF16), ((tk, d), BF16), ((tm, d), F32), ((tm, d), F32)]
    scratch = [((tm, d), F32)]
    return pl.pallas_call(
        _ffn_down_kernel,
        grid=(m // tm, dff // tk),
        in_specs=[pl.BlockSpec((tm, tk), lambda i, k: (i, k)), pl.BlockSpec((tk, d), lambda i, k: (k, 0)),
                  row, pl.BlockSpec((1, d), lambda i, k: (0, 0))],
        out_specs=row,
        out_shape=jax.ShapeDtypeStruct((m, d), F32),
        scratch_shapes=[pltpu.VMEM(s, dt) for s, dt in scratch],
        compiler_params=_params(("parallel", "arbitrary"), blocks, scratch),
        name="ffn_down",
    )(hd, w_down, x1, g_post)


def _ple(x2, pe, w_gate, w_proj):
    m, d = x2.shape
    pd = pe.shape[1]
    tm, tn = _tile(m, ROW_TILE // 2), _tile(d, COL_TILE)
    blocks = [((tm, d), F32), ((tm, pd), F32), ((d, tn), BF16), ((pd, tn), BF16), ((tm, tn), F32)]
    scratch = [((tm, d), BF16)]
    return pl.pallas_call(
        _ple_kernel,
        grid=(m // tm, d // tn),
        in_specs=[_row_spec(tm, d), _row_spec(tm, pd), _col_spec(d, tn), _col_spec(pd, tn)],
        out_specs=_tile_spec(tm, tn),
        out_shape=jax.ShapeDtypeStruct((m, d), F32),
        scratch_shapes=[pltpu.VMEM(s, dt) for s, dt in scratch],
        compiler_params=_params(("parallel", "arbitrary"), blocks, scratch),
        name="ple",
    )(x2, pe, w_gate, w_proj)


def _lambda_specs(dh, index_map):
    return [pl.BlockSpec((1, dh), index_map)] * 4


def _prompt_attention(q, k, v, slopes, lams, g_subln, lam_init, n_heads):
    b, t, aw = q.shape
    e = aw // n_heads
    dh = e // 2
    assert e == LANES and dh >= 3 * BF16_PIECES, "ALiBi lanes ride in the unused half of each map's copy"
    tq = _tile(t, ATTN_TILE)
    const = lambda bi, h, qi: (0, 0)
    seq = pl.BlockSpec((1, t, e), lambda bi, h, qi: (bi, 0, h))
    blk = pl.BlockSpec((1, tq, e), lambda bi, h, qi: (bi, qi, h))
    blocks = [((t, e), F32), ((t, e), F32), ((tq, e), BF16), ((tq, e), BF16)]
    scratch = [((t, e), BF16), ((t, e), BF16), ((t, e), BF16), ((2 * tq, tq), F32)]
    return pl.pallas_call(
        functools.partial(_prompt_attn_kernel, lam_init=lam_init, dh=dh),
        grid=(b, n_heads, t // tq),
        in_specs=[pl.BlockSpec(memory_space=pltpu.SMEM)] + _lambda_specs(dh, const)
        + [pl.BlockSpec((1, e), const), blk, seq, seq],
        out_specs=blk,
        out_shape=jax.ShapeDtypeStruct((b, t, aw), BF16),
        scratch_shapes=[pltpu.VMEM(s, d) for s, d in scratch],
        compiler_params=_params(("parallel", "parallel", "arbitrary"), blocks, scratch),
        name="prompt_attention",
    )(slopes, *lams, g_subln, q, k, v)


def _decode_attention(q, k_new, v_new, cache_k, cache_v, page_table, slopes, lams, g_subln,
                      lam_init, n_heads):
    bs, aw = q.shape
    n_pool, ps = cache_k.shape[:2]
    n_pages = page_table.shape[1]
    e = aw // n_heads
    dh = e // 2
    group = math.gcd(n_pages, DECODE_PAGES)
    rows, cols = 2 * n_heads, ps * n_heads
    const = lambda bi, p, pt: (0, 0)
    tok = pl.BlockSpec((1, n_heads, e), lambda bi, p, pt: (bi, 0, 0))
    pages = [pl.BlockSpec((1, cols, e), lambda bi, p, pt, g=g: (pt[bi * n_pages + p * group + g], 0, 0))
             for g in range(group)]
    col = jnp.arange(group * cols, dtype=jnp.int32)[None, :]
    blocks = [((cols, e), F32)] * (2 * group)
    scratch = [((rows, e), BF16), ((rows, group * cols), F32), ((rows, 1), F32), ((rows, 1), F32),
               ((rows, e), F32)]
    grid_spec = pltpu.PrefetchScalarGridSpec(
        num_scalar_prefetch=1,
        grid=(bs, n_pages // group),
        in_specs=[pl.BlockSpec((rows, 1), const), pl.BlockSpec((1, group * cols), const),
                  pl.BlockSpec((1, group * cols), const)]
        + _lambda_specs(dh, const) + [pl.BlockSpec((1, e), const), tok, tok, tok] + pages + pages,
        out_specs=tok,
        scratch_shapes=[pltpu.VMEM(s, d) for s, d in scratch],
    )
    pool_k = cache_k.reshape(n_pool, cols, e)
    pool_v = cache_v.reshape(n_pool, cols, e)
    o = pl.pallas_call(
        functools.partial(_decode_attn_kernel, lam_init=lam_init, dh=dh, group=group),
        grid_spec=grid_spec,
        out_shape=jax.ShapeDtypeStruct((bs, n_heads, e), BF16),
        compiler_params=_params(("parallel", "arbitrary"), blocks, scratch),
        name="decode_attention",
    )(page_table.reshape(-1), (jnp.tile(slopes, 2) * LOG2E)[:, None], col % n_heads, (col // n_heads).astype(F32),
      *lams, g_subln, q.reshape(bs, n_heads, e), k_new.reshape(bs, n_heads, e), v_new.reshape(bs, n_heads, e),
      *([pool_k] * group), *([pool_v] * group))
    return o.reshape(bs, aw)


def _prompt_conv(u, conv_w, conv_b, g, b):
    bsz, t, ch = u.shape
    width = conv_w.shape[0]
    hist = -(-(width - 1) // SUBLANES) * SUBLANES
    tt = _tile(t, CONV_ROWS)
    assert tt % hist == 0
    const = lambda bi, ti: (0, 0)
    vec = pl.BlockSpec((1, ch), const)
    blocks = [((hist, ch), F32), ((tt, ch), F32), ((width, ch), F32), ((tt, ch), BF16)]
    scratch = [((hist + tt, ch), F32), ((tt, ch), F32)]
    return pl.pallas_call(
        _prompt_conv_kernel,
        grid=(bsz, t // tt),
        in_specs=[pl.BlockSpec((1, hist, ch), lambda bi, ti: (bi, jnp.maximum(ti * (tt // hist) - 1, 0), 0)),
                  pl.BlockSpec((1, tt, ch), lambda bi, ti: (bi, ti, 0)),
                  pl.BlockSpec((width, ch), const), vec, vec, vec],
        out_specs=pl.BlockSpec((1, tt, ch), lambda bi, ti: (bi, ti, 0)),
        out_shape=jax.ShapeDtypeStruct((bsz, t, ch), BF16),
        scratch_shapes=[pltpu.VMEM(s, d) for s, d in scratch],
        compiler_params=_params(("parallel", "arbitrary"), blocks, scratch),
        name="prompt_conv",
    )(u, u, conv_w, conv_b, g, b)


def _sample_conv(state, u, conv_w, conv_b, g, b):
    hist, bs, ch = state.shape
    width = conv_w.shape[0]
    assert hist == width - 1
    bt = _tile(bs, 4 * SUBLANES) if bs % SUBLANES == 0 else bs
    vec = pl.BlockSpec((1, ch), lambda i: (0, 0))
    st = pl.BlockSpec((hist, bt, ch), lambda i: (0, i, 0))
    blocks = [((hist, bt, ch), F32), ((bt, ch), F32), ((width, ch), F32), ((bt, ch), BF16), ((hist, bt, ch), F32)]
    return pl.pallas_call(
        _sample_conv_kernel,
        grid=(bs // bt,),
        in_specs=[st, pl.BlockSpec((bt, ch), lambda i: (i, 0)), pl.BlockSpec((width, ch), lambda i: (0, 0)),
                  vec, vec, vec],
        out_specs=[pl.BlockSpec((bt, ch), lambda i: (i, 0)), st],
        out_shape=[jax.ShapeDtypeStruct((bs, ch), BF16), jax.ShapeDtypeStruct((hist, bs, ch), F32)],
        compiler_params=_params(("parallel",), blocks),
        name="sample_conv",
    )(state, u, conv_w, conv_b, g, b)


def _layer(x, pe, attend, conv_branch, lw, dims):
    aw, cd, d = dims["attn_width"], dims["conv_dim"], x.shape[1]
    w_in = lw["w_in"]
    h, q = _norm_q(x, lw["g_pre_mix"], w_in, aw, dims["attn_scale"])
    k_f = _proj(h, w_in, aw, aw)
    v_f = _proj(h, w_in, 2 * aw, aw)
    u = _glu(h, w_in, 3 * aw, cd)
    o = attend(q, k_f, v_f)
    c, conv_state = conv_branch(u)
    mix_in = _merge(h, o, c, w_in, 3 * aw + 2 * cd, lw["w_attn_out"], lw["w_conv_out"])
    x1, f = _out_proj(mix_in, lw["w_out"], x, lw["g_post_mix"], lw["g_pre_ffn"])
    hd = _ffn_up(f, lw["w_ffn_gate"], lw["w_ffn_up"])
    x2 = _ffn_down(hd, lw["w_ffn_down"], x1, lw["g_post_ffn"])
    y = _ple(x2, pe, lw["w_ple_gate"], lw["w_ple_proj"])
    return y, k_f, v_f, conv_state


_MATRICES = ("w_in", "w_attn_out", "w_conv_out", "w_out", "w_ffn_gate", "w_ffn_up", "w_ffn_down",
             "w_ple_proj", "w_ple_gate")
_VECTORS = ("g_pre_mix", "g_subln", "conv_b", "g_conv_norm", "b_conv_norm", "g_post_mix", "g_pre_ffn",
            "g_post_ffn", "lambda_q1", "lambda_k1", "lambda_q2", "lambda_k2")


def kernel(x_prompt, x_sample, cache_k, cache_v, state_conv, page_table, p_prompt, p_sample, g_pre_mix, w_in, lambda_q1, lambda_k1, lambda_q2, lambda_k2, g_subln, w_attn_out, conv_w, conv_b, g_conv_norm, b_conv_norm, w_conv_out, w_out, g_post_mix, g_pre_ffn, w_ffn_gate, w_ffn_up, w_ffn_down, g_post_ffn, w_ple_proj, w_ple_gate):
    weights = dict(g_pre_mix=g_pre_mix, w_in=w_in, lambda_q1=lambda_q1, lambda_k1=lambda_k1,
                   lambda_q2=lambda_q2, lambda_k2=lambda_k2, g_subln=g_subln, w_attn_out=w_attn_out,
                   conv_w=conv_w, conv_b=conv_b, g_conv_norm=g_conv_norm, b_conv_norm=b_conv_norm,
                   w_conv_out=w_conv_out, w_out=w_out, g_post_mix=g_post_mix, g_pre_ffn=g_pre_ffn,
                   w_ffn_gate=w_ffn_gate, w_ffn_up=w_ffn_up, w_ffn_down=w_ffn_down, g_post_ffn=g_post_ffn,
                   w_ple_proj=w_ple_proj, w_ple_gate=w_ple_gate)
    depth = w_in.shape[0]
    bp, t, d = x_prompt.shape
    bs, ts, _ = x_sample.shape
    assert ts == 1, "the sample group decodes one token per sequence"
    n_heads, e = cache_k.shape[3], cache_k.shape[4]
    dims = dict(attn_width=n_heads * e, conv_dim=conv_w.shape[2], attn_scale=(e // 2) ** -0.5 * LOG2E)
    slopes = 2.0 ** (-8.0 * jnp.arange(1, n_heads + 1, dtype=F32) / n_heads)

    yp = x_prompt.reshape(bp * t, d)
    ys = x_sample.reshape(bs, d)
    n_pool = cache_k.shape[1]
    pool_k = cache_k.reshape((depth * n_pool,) + cache_k.shape[2:])
    pool_v = cache_v.reshape((depth * n_pool,) + cache_v.shape[2:])
    outs = [[] for _ in range(6)]
    for i in range(depth):
        lam_init = 0.8 - 0.6 * math.exp(-0.3 * i)
        lw = {n: weights[n][i].astype(BF16) for n in _MATRICES}
        lw.update({n: weights[n][i][None, :] for n in _VECTORS})
        lw["conv_w"] = conv_w[i]
        lams = [lw[n] for n in ("lambda_q1", "lambda_k1", "lambda_q2", "lambda_k2")]
        ln = (lw["conv_b"], lw["g_conv_norm"], lw["b_conv_norm"])

        def attend_p(q, k, v):
            shape = (bp, t, dims["attn_width"])
            o = _prompt_attention(q.reshape(shape), k.reshape(shape), v.reshape(shape), slopes, lams,
                                  lw["g_subln"], lam_init, n_heads)
            return o.reshape(bp * t, -1)

        def conv_p(u):
            u3 = u.reshape(bp, t, -1)
            c = _prompt_conv(u3, lw["conv_w"], *ln)
            return c.reshape(bp * t, -1), u3[:, t - (conv_w.shape[1] - 1):, :]

        def attend_s(q, k, v):
            return _decode_attention(q, k, v, pool_k, pool_v, page_table + i * n_pool, slopes, lams,
                                     lw["g_subln"], lam_init, n_heads)

        def conv_s(u):
            c, state = _sample_conv(jnp.swapaxes(state_conv[i], 0, 1), u, lw["conv_w"], *ln)
            return c, jnp.swapaxes(state, 0, 1)

        yp, kp, vp, cp = _layer(yp, p_prompt[i].reshape(bp * t, -1), attend_p, conv_p, lw, dims)
        ys, ksn, vsn, csn = _layer(ys, p_sample[i].reshape(bs, -1), attend_s, conv_s, lw, dims)
        for lst, val in zip(outs, (kp.reshape(bp, t, n_heads, e), vp.reshape(bp, t, n_heads, e), cp,
                                   ksn.reshape(bs, 1, n_heads, e), vsn.reshape(bs, 1, n_heads, e), csn)):
            lst.append(val)
    stack = (lambda o: o[0][None]) if depth == 1 else jnp.stack
    return (yp.reshape(bp, t, d), ys.reshape(bs, 1, d), *[stack(o) for o in outs])
```

```python
import functools
import math

import jax
import jax.numpy as jnp
from jax import lax
from jax.experimental import pallas as pl
from jax.experimental.pallas import tpu as pltpu

F32 = jnp.float32
BF16 = jnp.bfloat16

EPS = 1e-6
NEG_INF = -1e30

LANES = 128
SUBLANES = 8
VMEM_BYTES_V7X = 64 * 1024 * 1024
VMEM_CEILING = VMEM_BYTES_V7X - 8 * 1024 * 1024
VMEM_INTERNAL = 12 * 1024 * 1024

ROW_TILE = 1024
COL_TILE = 512
NORM_ROWS = 256
ATTN_TILE = 512
CONV_ROWS = 128
DECODE_PAGES = 4

LOG2E = 1.4426950408889634
BF16_EXACT_INT = 256
BF16_PIECES = 3


def _tile(n, pref):
    if n <= pref:
        return n
    if pref < LANES:
        assert n % pref == 0, (n, pref)
        return pref
    t = (pref // LANES) * LANES
    while t > LANES and n % t:
        t -= LANES
    assert n % t == 0, (n, pref)
    return t


def _nbytes(shape, dtype):
    return math.prod(shape) * jnp.dtype(dtype).itemsize


def _params(semantics, blocks, scratch=()):
    need = 2 * sum(_nbytes(s, d) for s, d in blocks)
    need += sum(_nbytes(s, d) for s, d in scratch) + VMEM_INTERNAL
    return pltpu.CompilerParams(dimension_semantics=semantics,
                                vmem_limit_bytes=min(need, VMEM_CEILING))


def _dot(a, b):
    return jnp.dot(a, b, preferred_element_type=F32)


def _dot_nt(a, b):
    return lax.dot_general(a, b, (((1,), (1,)), ((), ())), preferred_element_type=F32)


def _rms(x, g):
    return x * lax.rsqrt(jnp.mean(x * x, axis=-1, keepdims=True) + EPS) * g


def _lambda(lq1, lk1, lq2, lk2, lam_init):
    s1 = jnp.sum(lq1[...] * lk1[...], axis=-1, keepdims=True)
    s2 = jnp.sum(lq2[...] * lk2[...], axis=-1, keepdims=True)
    return jnp.exp(s1) - jnp.exp(s2) + lam_init


def _in_proj_kernel(x_ref, g_ref, wa_ref, wb_ref, h_ref, q_ref, k_ref, v_ref, u_ref, *, scale, n_attn):
    j = pl.program_id(1)

    @pl.when(j == 0)
    def _():
        rows = _tile(x_ref.shape[0], NORM_ROWS)
        for r in range(0, x_ref.shape[0], rows):
            h_ref[pl.ds(r, rows), :] = _rms(x_ref[pl.ds(r, rows), :], g_ref[...]).astype(BF16)

    @pl.when(j < n_attn)
    def _():
        q_ref[...] = (_dot(h_ref[...], wa_ref[...]) * scale).astype(BF16)

    @pl.when((j >= n_attn) & (j < 2 * n_attn))
    def _():
        k_ref[...] = _dot(h_ref[...], wa_ref[...])

    @pl.when((j >= 2 * n_attn) & (j < 3 * n_attn))
    def _():
        v_ref[...] = _dot(h_ref[...], wa_ref[...])

    @pl.when(j >= 3 * n_attn)
    def _():
        h = h_ref[...]
        u_ref[...] = _dot(h, wa_ref[...]) * jax.nn.sigmoid(_dot(h, wb_ref[...]))


def _merge_kernel(h_ref, o_ref, c_ref, wga_ref, wgc_ref, wa_ref, wc_ref, m_ref):
    h = h_ref[...]
    a = _dot(o_ref[...], wa_ref[...])
    c = _dot(c_ref[...], wc_ref[...])
    m = jax.nn.sigmoid(_dot(h, wga_ref[...])) * a + jax.nn.sigmoid(_dot(h, wgc_ref[...])) * c
    m_ref[...] = m.astype(BF16)


def _out_kernel(m_ref, w_ref, x_ref, gpost_ref, gpre_ref, x1_ref, f_ref):
    mix = _dot(m_ref[...], w_ref[...])
    x1 = x_ref[...] + _rms(mix, gpost_ref[...])
    x1_ref[...] = x1
    f_ref[...] = _rms(x1, gpre_ref[...]).astype(BF16)


def _ffn_up_kernel(f_ref, wg_ref, wu_ref, hd_ref):
    f = f_ref[...]
    hd_ref[...] = (jax.nn.silu(_dot(f, wg_ref[...])) * _dot(f, wu_ref[...])).astype(BF16)


def _ffn_down_kernel(hd_ref, w_ref, x1_ref, g_ref, x2_ref, y_ref):
    j = pl.program_id(1)
    tn = w_ref.shape[1]
    y_ref[:, pl.ds(pl.multiple_of(j * tn, tn), tn)] = _dot(hd_ref[...], w_ref[...])

    @pl.when(j == pl.num_programs(1) - 1)
    def _():
        rows = _tile(y_ref.shape[0], NORM_ROWS)
        for r in range(0, y_ref.shape[0], rows):
            sl = pl.ds(r, rows)
            x2_ref[sl, :] = x1_ref[sl, :] + _rms(y_ref[sl, :], g_ref[...])


def _ple_kernel(x2_ref, pe_ref, wg_ref, wp_ref, y_ref, x2b_ref):
    j = pl.program_id(1)
    tn = y_ref.shape[1]

    @pl.when(j == 0)
    def _():
        x2b_ref[...] = x2_ref[...].astype(BF16)

    gate = jax.nn.sigmoid(_dot(x2b_ref[...], wg_ref[...]))
    res = x2_ref[:, pl.ds(pl.multiple_of(j * tn, tn), tn)]
    y_ref[...] = res + gate * _dot(pe_ref[...].astype(BF16), wp_ref[...])


def _bf16_pieces(x):
    pieces = []
    for _ in range(BF16_PIECES):
        p = x.astype(BF16).astype(F32)
        pieces.append(p)
        x = x - p
    return pieces


def _alibi_lanes(lane0, lane, groups):
    out = jnp.zeros(lane.shape, F32)
    for g, pieces in enumerate(groups):
        for i, p in enumerate(pieces):
            out = jnp.where(lane == lane0 + g * BF16_PIECES + i, p, out)
    return out


def _prompt_attn_kernel(slopes_ref, lq1, lk1, lq2, lk2, gsub_ref, kx1_ref, kx2_ref, mask_ref, q_ref, k_ref,
                        v_ref, o_ref, k1_ref, k2_ref, vt_ref, sa_ref, sb_ref, p_ref, acc_ref, *, lam_init, dh, t):
    seq, e = k_ref.shape[1], k_ref.shape[2]
    n_blocks = seq // t
    sigma = slopes_ref[pl.program_id(1)] * LOG2E

    k = k_ref[0].astype(BF16)
    lane = lax.broadcasted_iota(jnp.int32, k.shape, 1)
    k1_ref[...] = jnp.where(lane < dh, k, kx1_ref[...])
    k2_ref[...] = jnp.where(lane >= dh, k, kx2_ref[...])
    vt_ref[...] = v_ref[0].T.astype(BF16)
    lam = _lambda(lq1, lk1, lq2, lk2, lam_init)
    score_refs = (sa_ref, sb_ref)

    def queries(qi):
        q = q_ref[0, pl.ds(qi * t, t), :].astype(F32)
        lane = lax.broadcasted_iota(jnp.int32, (t, e), 1)
        qpos = (qi * t + lax.broadcasted_iota(jnp.int32, (t, e), 0)).astype(F32)
        sig = jnp.full((t, e), sigma, F32)
        sig_p = _bf16_pieces(sig)
        groups = [sig_p, sig_p, _bf16_pieces(-sig * qpos)]
        return (jnp.where(lane < dh, q, _alibi_lanes(dh, lane, groups)).astype(BF16),
                jnp.where(lane >= dh, q, _alibi_lanes(0, lane, groups)).astype(BF16))

    def scores(s_ref, qs, kj):
        keys = pl.ds(kj * t, t)
        s_ref[:, pl.ds(0, t)] = _dot_nt(k1_ref[keys, :], qs[0])
        s_ref[:, pl.ds(t, t)] = _dot_nt(k2_ref[keys, :], qs[1])

    def update(s_ref, kj, m, l, first, masked):
        m_out, l_out, alphas = [], [], []
        for c in range(2 * t // LANES):
            cols = pl.ds(c * LANES, LANES)
            sc = s_ref[:, cols]
            if masked:
                sc = sc + mask_ref[:, cols]
            mc = jnp.max(sc, axis=0, keepdims=True)
            if not first:
                mc = jnp.maximum(m[c], mc)
            p = jnp.exp2(sc - mc)
            lc = jnp.sum(p, axis=0, keepdims=True)
            if not first:
                alpha = jnp.exp2(m[c] - mc)
                lc = alpha * l[c] + lc
                alphas.append(alpha)
            p_ref[:, cols] = p.astype(BF16)
            m_out.append(mc)
            l_out.append(lc)
        pv = _dot(vt_ref[:, pl.ds(kj * t, t)], p_ref[...])
        if first:
            acc_ref[...] = pv
        else:
            acc_ref[...] = jnp.concatenate(alphas, axis=1) * acc_ref[...] + pv
        return m_out, l_out

    order = [(qi, kj) for qi in range(n_blocks) for kj in range(qi + 1)]
    qs = queries(0)
    scores(score_refs[0], qs, 0)
    m = l = None
    for idx, (qi, kj) in enumerate(order):
        if idx + 1 < len(order):
            nqi, nkj = order[idx + 1]
            nqs = queries(nqi) if nqi != qi else qs
            scores(score_refs[(idx + 1) % 2], nqs, nkj)
        m, l = update(score_refs[idx % 2], kj, m, l, first=(kj == 0), masked=(kj == qi))
        if kj == qi:
            l_row = jnp.concatenate(l, axis=1)
            acc = acc_ref[...]
            o = (acc[:, :t] / l_row[:, :t] - lam * (acc[:, t:] / l_row[:, t:])).T
            o_ref[0, pl.ds(qi * t, t), :] = (_rms(o, gsub_ref[...]) * (1.0 - lam_init)).astype(BF16)
        if idx + 1 < len(order):
            qs = nqs


def _decode_attn_kernel(pt_ref, sig_ref, khead_ref, kpos_ref, lq1, lk1, lq2, lk2, gsub_ref,
                        q_ref, kn_ref, vn_ref, *refs, lam_init, dh, group):
    del pt_ref
    kc_refs, vc_refs = refs[:group], refs[group:2 * group]
    o_ref, qs_ref, bias_ref, m_ref, l_ref, acc_ref = refs[2 * group:]
    step = pl.program_id(1)
    n_steps = pl.num_programs(1)
    n_heads = q_ref.shape[1]
    cols = kc_refs[0].shape[1]
    ps = cols // n_heads
    sigma = sig_ref[...]

    @pl.when(step == 0)
    def _():
        q = q_ref[0]
        lane = lax.broadcasted_iota(jnp.int32, q.shape, 1)
        zero = jnp.zeros_like(q)
        qs_ref[...] = jnp.concatenate([jnp.where(lane < dh, q, zero), jnp.where(lane >= dh, q, zero)], axis=0)
        row_head = lax.broadcasted_iota(jnp.int32, bias_ref.shape, 0) % n_heads
        bias_ref[...] = jnp.where(khead_ref[...] == row_head, sigma * kpos_ref[...], NEG_INF)
        m_ref[...] = jnp.full_like(m_ref, NEG_INF)
        l_ref[...] = jnp.zeros_like(l_ref)
        acc_ref[...] = jnp.zeros_like(acc_ref)

    shift = -sigma * ((n_steps - step) * (group * ps)).astype(F32)
    qs = qs_ref[...]
    s = jnp.concatenate([_dot_nt(qs, kc[0].astype(BF16)) for kc in kc_refs], axis=1) + bias_ref[...]
    m_old = m_ref[...]
    m_new = jnp.maximum(m_old, jnp.max(s, axis=-1, keepdims=True) + shift)
    p = jnp.exp2(s - (m_new - shift)).astype(BF16)
    alpha = jnp.exp2(m_old - m_new)
    m_ref[...] = m_new
    l_ref[...] = alpha * l_ref[...] + jnp.sum(p.astype(F32), axis=-1, keepdims=True)
    pv = _dot(p[:, :cols], vc_refs[0][0].astype(BF16))
    for g in range(1, group):
        pv = pv + _dot(p[:, g * cols:(g + 1) * cols], vc_refs[g][0].astype(BF16))
    acc_ref[...] = alpha * acc_ref[...] + pv

    @pl.when(step == n_steps - 1)
    def _():
        prod = q_ref[0].astype(F32) * kn_ref[0].astype(BF16).astype(F32)
        lane = lax.broadcasted_iota(jnp.int32, prod.shape, 1)
        s1 = jnp.sum(jnp.where(lane < dh, prod, 0.0), axis=-1, keepdims=True)
        s2 = jnp.sum(jnp.where(lane >= dh, prod, 0.0), axis=-1, keepdims=True)
        s_self = jnp.concatenate([s1, s2], axis=0)
        vn = vn_ref[0].astype(BF16).astype(F32)
        m_old = m_ref[...]
        m_fin = jnp.maximum(m_old, s_self)
        alpha = jnp.exp2(m_old - m_fin)
        p_self = jnp.exp2(s_self - m_fin)
        l = alpha * l_ref[...] + p_self
        acc = alpha * acc_ref[...] + p_self * jnp.concatenate([vn, vn], axis=0)
        a = acc / l
        lam = _lambda(lq1, lk1, lq2, lk2, lam_init)
        o = a[:n_heads] - lam * a[n_heads:]
        o_ref[0] = (_rms(o, gsub_ref[...]) * (1.0 - lam_init)).astype(BF16)


def _ln_silu(c, g, b):
    mu = jnp.mean(c, axis=-1, keepdims=True)
    var = jnp.mean(jnp.square(c - mu), axis=-1, keepdims=True)
    y = (c - mu) * lax.rsqrt(var + EPS) * g + b
    return jax.nn.silu(y)


def _prompt_conv_kernel(prev_ref, cur_ref, w_ref, cb_ref, g_ref, b_ref, o_ref, full_ref, c_ref, sh_ref):
    hist = prev_ref.shape[1]
    tt, ch = c_ref.shape
    width = w_ref.shape[0]

    @pl.when(pl.program_id(1) == 0)
    def _():
        full_ref[pl.ds(0, hist), :] = jnp.zeros((hist, ch), F32)

    @pl.when(pl.program_id(1) > 0)
    def _():
        full_ref[pl.ds(0, hist), :] = prev_ref[0]

    full_ref[pl.ds(hist, tt), :] = cur_ref[0]

    base = hist - (width - 1)

    def chunk(cc, _):
        lanes = pl.ds(pl.multiple_of(cc * LANES, LANES), LANES)
        acc = jnp.broadcast_to(cb_ref[:, lanes], (tt, LANES))
        for r in range(min(SUBLANES, width)):
            taps = range(r, width, SUBLANES)
            span = tt + taps[-1] - r
            sh_ref[r, pl.ds(0, span), :] = full_ref[pl.ds(base + r, span), lanes]
            for k in taps:
                acc = acc + w_ref[pl.ds(k, 1), lanes] * sh_ref[r, pl.ds(k - r, tt), :]
        c_ref[:, lanes] = acc
        return 0

    lax.fori_loop(0, ch // LANES, chunk, 0)
    o_ref[0] = _ln_silu(c_ref[...], g_ref[...], b_ref[...]).astype(BF16)


def _sample_conv_kernel(state_ref, u_ref, w_ref, cb_ref, g_ref, b_ref, o_ref, ns_ref):
    hist = state_ref.shape[0]
    u = u_ref[...]
    c = u * w_ref[pl.ds(hist, 1), :] + cb_ref[...]
    for k in range(hist):
        c = c + state_ref[k] * w_ref[pl.ds(k, 1), :]
    o_ref[...] = _ln_silu(c, g_ref[...], b_ref[...]).astype(BF16)
    for k in range(hist - 1):
        ns_ref[k] = state_ref[k + 1]
    ns_ref[hist - 1] = u


def _row_spec(tm, n):
    return pl.BlockSpec((tm, n), lambda i, j: (i, 0))


def _col_spec(k, tn, off_blocks=0):
    return pl.BlockSpec((k, tn), lambda i, j: (0, j + off_blocks))


def _tile_spec(tm, tn):
    return pl.BlockSpec((tm, tn), lambda i, j: (i, j))


def _vec_spec(n):
    return pl.BlockSpec((1, n), lambda i, j: (0, 0))


def _in_proj(x, g, w_in, aw, cd, scale):
    m, d = x.shape
    tm, tn = _tile(m, ROW_TILE), _tile(math.gcd(aw, cd), COL_TILE)
    na, nu = aw // tn, cd // tn
    gate0 = (3 * aw + cd) // tn

    def out_tile(first, count):
        return pl.BlockSpec((tm, tn), lambda i, j: (i, jnp.clip(j - first, 0, count - 1)))

    blocks = [((tm, d), F32), ((1, d), F32), ((d, tn), BF16), ((d, tn), BF16), ((tm, d), BF16),
              ((tm, tn), BF16), ((tm, tn), F32), ((tm, tn), F32), ((tm, tn), F32)]
    return pl.pallas_call(
        functools.partial(_in_proj_kernel, scale=scale, n_attn=na),
        grid=(m // tm, 3 * na + nu),
        in_specs=[_row_spec(tm, d), _vec_spec(d), _col_spec(d, tn),
                  pl.BlockSpec((d, tn), lambda i, j: (0, gate0 + jnp.maximum(j - 3 * na, 0)))],
        out_specs=[_row_spec(tm, d), out_tile(0, na), out_tile(na, na), out_tile(2 * na, na),
                   out_tile(3 * na, nu)],
        out_shape=[jax.ShapeDtypeStruct((m, d), BF16), jax.ShapeDtypeStruct((m, aw), BF16),
                   jax.ShapeDtypeStruct((m, aw), F32), jax.ShapeDtypeStruct((m, aw), F32),
                   jax.ShapeDtypeStruct((m, cd), F32)],
        compiler_params=_params(("parallel", "arbitrary"), blocks),
        name="in_proj",
    )(x, g, w_in, w_in)


def _merge(h, o, c, w_in, off, w_attn_out, w_conv_out):
    m, d = h.shape
    aw, cd = o.shape[1], c.shape[1]
    tm, tn = _tile(m, ROW_TILE), _tile(d, COL_TILE // 2)
    assert off % tn == 0
    blocks = [((tm, d), BF16), ((tm, aw), BF16), ((tm, cd), BF16), ((d, tn), BF16), ((d, tn), BF16),
              ((aw, tn), BF16), ((cd, tn), BF16), ((tm, tn), BF16)]
    return pl.pallas_call(
        _merge_kernel,
        grid=(m // tm, d // tn),
        in_specs=[_row_spec(tm, d), _row_spec(tm, aw), _row_spec(tm, cd),
                  _col_spec(d, tn, off // tn), _col_spec(d, tn, (off + d) // tn),
                  _col_spec(aw, tn), _col_spec(cd, tn)],
        out_specs=_tile_spec(tm, tn),
        out_shape=jax.ShapeDtypeStruct((m, d), BF16),
        compiler_params=_params(("parallel", "arbitrary"), blocks),
        name="gated_merge",
    )(h, o, c, w_in, w_in, w_attn_out, w_conv_out)


def _out_proj(mix_in, w_out, x, g_post, g_pre):
    m, d = x.shape
    tm = _tile(m, ROW_TILE // 4)
    row = pl.BlockSpec((tm, d), lambda i: (i, 0))
    vec = pl.BlockSpec((1, d), lambda i: (0, 0))
    blocks = [((tm, d), BF16), ((d, d), BF16), ((tm, d), F32), ((tm, d), F32), ((tm, d), BF16)]
    return pl.pallas_call(
        _out_kernel,
        grid=(m // tm,),
        in_specs=[row, pl.BlockSpec((d, d), lambda i: (0, 0)), row, vec, vec],
        out_specs=[row, row],
        out_shape=[jax.ShapeDtypeStruct((m, d), F32), jax.ShapeDtypeStruct((m, d), BF16)],
        compiler_params=_params(("parallel",), blocks),
        name="out_proj",
    )(mix_in, w_out, x, g_post, g_pre)


def _ffn_up(f, w_gate, w_up):
    m, d = f.shape
    dff = w_gate.shape[1]
    tm, tn = _tile(m, ROW_TILE), _tile(dff, COL_TILE)
    blocks = [((tm, d), BF16), ((d, tn), BF16), ((d, tn), BF16), ((tm, tn), BF16)]
    return pl.pallas_call(
        _ffn_up_kernel,
        grid=(m // tm, dff // tn),
        in_specs=[_row_spec(tm, d), _col_spec(d, tn), _col_spec(d, tn)],
        out_specs=_tile_spec(tm, tn),
        out_shape=jax.ShapeDtypeStruct((m, dff), BF16),
        compiler_params=_params(("parallel", "arbitrary"), blocks),
        name="ffn_up",
    )(f, w_gate, w_up)


def _ffn_down(hd, w_down, x1, g_post):
    m, dff = hd.shape
    d = x1.shape[1]
    tm, tn = _tile(m, ROW_TILE // 2), _tile(d, COL_TILE)
    row = pl.BlockSpec((tm, d), lambda i, j: (i, 0))
    blocks = [((tm, dff), BF16), ((dff, tn), BF16), ((tm, d), F32), ((tm, d), F32)]
    scratch = [((tm, d), F32)]
    return pl.pallas_call(
        _ffn_down_kernel,
        grid=(m // tm, d // tn),
        in_specs=[_row_spec(tm, dff), _col_spec(dff, tn), row, _vec_spec(d)],
        out_specs=row,
        out_shape=jax.ShapeDtypeStruct((m, d), F32),
        scratch_shapes=[pltpu.VMEM(s, dt) for s, dt in scratch],
        compiler_params=_params(("parallel", "arbitrary"), blocks, scratch),
        name="ffn_down",
    )(hd, w_down, x1, g_post)


def _ple(x2, pe, w_gate, w_proj):
    m, d = x2.shape
    pd = pe.shape[1]
    tm, tn = _tile(m, ROW_TILE), _tile(d, COL_TILE)
    blocks = [((tm, d), F32), ((tm, pd), F32), ((d, tn), BF16), ((pd, tn), BF16), ((tm, tn), F32)]
    scratch = [((tm, d), BF16)]
    return pl.pallas_call(
        _ple_kernel,
        grid=(m // tm, d // tn),
        in_specs=[_row_spec(tm, d), _row_spec(tm, pd), _col_spec(d, tn), _col_spec(pd, tn)],
        out_specs=_tile_spec(tm, tn),
        out_shape=jax.ShapeDtypeStruct((m, d), F32),
        scratch_shapes=[pltpu.VMEM(s, dt) for s, dt in scratch],
        compiler_params=_params(("parallel", "arbitrary"), blocks, scratch),
        name="ple",
    )(x2, pe, w_gate, w_proj)


def _lambda_specs(dh, index_map):
    return [pl.BlockSpec((1, dh), index_map)] * 4


def _alibi_key_lanes(t, e, lane0):
    pos = jnp.arange(t, dtype=jnp.int32)[:, None]
    lo = pos % BF16_EXACT_INT
    g = (jnp.arange(e, dtype=jnp.int32)[None, :] - lane0) // BF16_PIECES
    vals = jnp.where(g == 0, pos - lo, jnp.where(g == 1, lo, 1))
    return jnp.where((g >= 0) & (g < 3), vals, 0).astype(BF16)


def _prompt_attention(q, k, v, slopes, lams, g_subln, lam_init, n_heads):
    b, t, aw = q.shape
    e = aw // n_heads
    dh = e // 2
    assert e == LANES and dh >= 3 * BF16_PIECES, "ALiBi lanes ride in the unused half of each map's copy"
    tq = _tile(t, ATTN_TILE)
    key = jnp.arange(tq, dtype=jnp.int32)[:, None]
    qry = jnp.arange(2 * tq, dtype=jnp.int32)[None, :] % tq
    mask = jnp.where(key <= qry, 0.0, NEG_INF).astype(F32)
    const = lambda bi, h: (0, 0)
    seq = pl.BlockSpec((1, t, e), lambda bi, h: (bi, 0, h))
    blocks = [((t, e), F32), ((t, e), F32), ((t, e), BF16), ((t, e), BF16), ((t, e), BF16), ((t, e), BF16),
              ((tq, 2 * tq), F32)]
    scratch = [((t, e), BF16), ((t, e), BF16), ((e, t), BF16), ((tq, 2 * tq), F32), ((tq, 2 * tq), F32),
               ((tq, 2 * tq), BF16), ((e, 2 * tq), F32)]
    return pl.pallas_call(
        functools.partial(_prompt_attn_kernel, lam_init=lam_init, dh=dh, t=tq),
        grid=(b, n_heads),
        in_specs=[pl.BlockSpec(memory_space=pltpu.SMEM)] + _lambda_specs(dh, const)
        + [pl.BlockSpec((1, e), const), pl.BlockSpec((t, e), const), pl.BlockSpec((t, e), const),
           pl.BlockSpec((tq, 2 * tq), const), seq, seq, seq],
        out_specs=seq,
        out_shape=jax.ShapeDtypeStruct((b, t, aw), BF16),
        scratch_shapes=[pltpu.VMEM(s, d) for s, d in scratch],
        compiler_params=_params(("parallel", "parallel"), blocks, scratch),
        name="prompt_attention",
    )(slopes, *lams, g_subln, _alibi_key_lanes(t, e, dh), _alibi_key_lanes(t, e, 0), mask, q, k, v)


def _decode_attention(q, k_new, v_new, cache_k, cache_v, page_table, slopes, lams, g_subln,
                      lam_init, n_heads):
    bs, aw = q.shape
    n_pool, ps = cache_k.shape[:2]
    n_pages = page_table.shape[1]
    e = aw // n_heads
    dh = e // 2
    group = math.gcd(n_pages, DECODE_PAGES)
    rows, cols = 2 * n_heads, ps * n_heads
    const = lambda bi, p, pt: (0, 0)
    tok = pl.BlockSpec((1, n_heads, e), lambda bi, p, pt: (bi, 0, 0))
    pages = [pl.BlockSpec((1, cols, e), lambda bi, p, pt, g=g: (pt[bi * n_pages + p * group + g], 0, 0))
             for g in range(group)]
    col = jnp.arange(group * cols, dtype=jnp.int32)[None, :]
    blocks = [((cols, e), F32)] * (2 * group)
    scratch = [((rows, e), BF16), ((rows, group * cols), F32), ((rows, 1), F32), ((rows, 1), F32),
               ((rows, e), F32)]
    grid_spec = pltpu.PrefetchScalarGridSpec(
        num_scalar_prefetch=1,
        grid=(bs, n_pages // group),
        in_specs=[pl.BlockSpec((rows, 1), const), pl.BlockSpec((1, group * cols), const),
                  pl.BlockSpec((1, group * cols), const)]
        + _lambda_specs(dh, const) + [pl.BlockSpec((1, e), const), tok, tok, tok] + pages + pages,
        out_specs=tok,
        scratch_shapes=[pltpu.VMEM(s, d) for s, d in scratch],
    )
    pool_k = cache_k.reshape(n_pool, cols, e)
    pool_v = cache_v.reshape(n_pool, cols, e)
    o = pl.pallas_call(
        functools.partial(_decode_attn_kernel, lam_init=lam_init, dh=dh, group=group),
        grid_spec=grid_spec,
        out_shape=jax.ShapeDtypeStruct((bs, n_heads, e), BF16),
        compiler_params=_params(("parallel", "arbitrary"), blocks, scratch),
        name="decode_attention",
    )(page_table.reshape(-1), (jnp.tile(slopes, 2) * LOG2E)[:, None], col % n_heads, (col // n_heads).astype(F32),
      *lams, g_subln, q.reshape(bs, n_heads, e), k_new.reshape(bs, n_heads, e), v_new.reshape(bs, n_heads, e),
      *([pool_k] * group), *([pool_v] * group))
    return o.reshape(bs, aw)


def _prompt_conv(u, conv_w, conv_b, g, b):
    bsz, t, ch = u.shape
    width = conv_w.shape[0]
    hist = -(-(width - 1) // SUBLANES) * SUBLANES
    tt = _tile(t, CONV_ROWS)
    assert tt % hist == 0
    const = lambda bi, ti: (0, 0)
    vec = pl.BlockSpec((1, ch), const)
    blocks = [((hist, ch), F32), ((tt, ch), F32), ((width, ch), F32), ((tt, ch), BF16)]
    scratch = [((hist + tt, ch), F32), ((tt, ch), F32), ((SUBLANES, hist + tt, LANES), F32)]
    return pl.pallas_call(
        _prompt_conv_kernel,
        grid=(bsz, t // tt),
        in_specs=[pl.BlockSpec((1, hist, ch), lambda bi, ti: (bi, jnp.maximum(ti * (tt // hist) - 1, 0), 0)),
                  pl.BlockSpec((1, tt, ch), lambda bi, ti: (bi, ti, 0)),
                  pl.BlockSpec((width, ch), const), vec, vec, vec],
        out_specs=pl.BlockSpec((1, tt, ch), lambda bi, ti: (bi, ti, 0)),
        out_shape=jax.ShapeDtypeStruct((bsz, t, ch), BF16),
        scratch_shapes=[pltpu.VMEM(s, d) for s, d in scratch],
        compiler_params=_params(("parallel", "arbitrary"), blocks, scratch),
        name="prompt_conv",
    )(u, u, conv_w, conv_b, g, b)


def _sample_conv(state, u, conv_w, conv_b, g, b):
    hist, bs, ch = state.shape
    width = conv_w.shape[0]
    assert hist == width - 1
    bt = _tile(bs, 4 * SUBLANES) if bs % SUBLANES == 0 else bs
    vec = pl.BlockSpec((1, ch), lambda i: (0, 0))
    st = pl.BlockSpec((hist, bt, ch), lambda i: (0, i, 0))
    blocks = [((hist, bt, ch), F32), ((bt, ch), F32), ((width, ch), F32), ((bt, ch), BF16), ((hist, bt, ch), F32)]
    return pl.pallas_call(
        _sample_conv_kernel,
        grid=(bs // bt,),
        in_specs=[st, pl.BlockSpec((bt, ch), lambda i: (i, 0)), pl.BlockSpec((width, ch), lambda i: (0, 0)),
                  vec, vec, vec],
        out_specs=[pl.BlockSpec((bt, ch), lambda i: (i, 0)), st],
        out_shape=[jax.ShapeDtypeStruct((bs, ch), BF16), jax.ShapeDtypeStruct((hist, bs, ch), F32)],
        compiler_params=_params(("parallel",), blocks),
        name="sample_conv",
    )(state, u, conv_w, conv_b, g, b)


def _layer(x, pe, attend, conv_branch, lw, dims):
    aw, cd, d = dims["attn_width"], dims["conv_dim"], x.shape[1]
    w_in = lw["w_in"]
    h, q, k_f, v_f, u = _in_proj(x, lw["g_pre_mix"], w_in, aw, cd, dims["attn_scale"])
    o = attend(q, k_f, v_f)
    c, conv_state = conv_branch(u)
    mix_in = _merge(h, o, c, w_in, 3 * aw + 2 * cd, lw["w_attn_out"], lw["w_conv_out"])
    x1, f = _out_proj(mix_in, lw["w_out"], x, lw["g_post_mix"], lw["g_pre_ffn"])
    hd = _ffn_up(f, lw["w_ffn_gate"], lw["w_ffn_up"])
    x2 = _ffn_down(hd, lw["w_ffn_down"], x1, lw["g_post_ffn"])
    y = _ple(x2, pe, lw["w_ple_gate"], lw["w_ple_proj"])
    return y, k_f, v_f, conv_state


_MATRICES = ("w_in", "w_attn_out", "w_conv_out", "w_out", "w_ffn_gate", "w_ffn_up", "w_ffn_down",
             "w_ple_proj", "w_ple_gate")
_VECTORS = ("g_pre_mix", "g_subln", "conv_b", "g_conv_norm", "b_conv_norm", "g_post_mix", "g_pre_ffn",
            "g_post_ffn", "lambda_q1", "lambda_k1", "lambda_q2", "lambda_k2")


def kernel(x_prompt, x_sample, cache_k, cache_v, state_conv, page_table, p_prompt, p_sample, g_pre_mix, w_in, lambda_q1, lambda_k1, lambda_q2, lambda_k2, g_subln, w_attn_out, conv_w, conv_b, g_conv_norm, b_conv_norm, w_conv_out, w_out, g_post_mix, g_pre_ffn, w_ffn_gate, w_ffn_up, w_ffn_down, g_post_ffn, w_ple_proj, w_ple_gate):
    weights = dict(g_pre_mix=g_pre_mix, w_in=w_in, lambda_q1=lambda_q1, lambda_k1=lambda_k1,
                   lambda_q2=lambda_q2, lambda_k2=lambda_k2, g_subln=g_subln, w_attn_out=w_attn_out,
                   conv_w=conv_w, conv_b=conv_b, g_conv_norm=g_conv_norm, b_conv_norm=b_conv_norm,
                   w_conv_out=w_conv_out, w_out=w_out, g_post_mix=g_post_mix, g_pre_ffn=g_pre_ffn,
                   w_ffn_gate=w_ffn_gate, w_ffn_up=w_ffn_up, w_ffn_down=w_ffn_down, g_post_ffn=g_post_ffn,
                   w_ple_proj=w_ple_proj, w_ple_gate=w_ple_gate)
    depth = w_in.shape[0]
    bp, t, d = x_prompt.shape
    bs, ts, _ = x_sample.shape
    assert ts == 1, "the sample group decodes one token per sequence"
    n_heads, e = cache_k.shape[3], cache_k.shape[4]
    dims = dict(attn_width=n_heads * e, conv_dim=conv_w.shape[2], attn_scale=(e // 2) ** -0.5 * LOG2E)
    slopes = 2.0 ** (-8.0 * jnp.arange(1, n_heads + 1, dtype=F32) / n_heads)

    yp = x_prompt.reshape(bp * t, d)
    ys = x_sample.reshape(bs, d)
    n_pool = cache_k.shape[1]
    pool_k = cache_k.reshape((depth * n_pool,) + cache_k.shape[2:])
    pool_v = cache_v.reshape((depth * n_pool,) + cache_v.shape[2:])
    outs = [[] for _ in range(6)]
    for i in range(depth):
        lam_init = 0.8 - 0.6 * math.exp(-0.3 * i)
        lw = {n: weights[n][i].astype(BF16) for n in _MATRICES}
        lw.update({n: weights[n][i][None, :] for n in _VECTORS})
        lw["conv_w"] = conv_w[i]
        lams = [lw[n] for n in ("lambda_q1", "lambda_k1", "lambda_q2", "lambda_k2")]
        ln = (lw["conv_b"], lw["g_conv_norm"], lw["b_conv_norm"])

        def attend_p(q, k, v):
            shape = (bp, t, dims["attn_width"])
            o = _prompt_attention(q.reshape(shape), k.reshape(shape), v.reshape(shape), slopes, lams,
                                  lw["g_subln"], lam_init, n_heads)
            return o.reshape(bp * t, -1)

        def conv_p(u):
            u3 = u.reshape(bp, t, -1)
            c = _prompt_conv(u3, lw["conv_w"], *ln)
            return c.reshape(bp * t, -1), u3[:, t - (conv_w.shape[1] - 1):, :]

        def attend_s(q, k, v):
            return _decode_attention(q, k, v, pool_k, pool_v, page_table + i * n_pool, slopes, lams,
                                     lw["g_subln"], lam_init, n_heads)

        def conv_s(u):
            c, state = _sample_conv(jnp.swapaxes(state_conv[i], 0, 1), u, lw["conv_w"], *ln)
            return c, jnp.swapaxes(state, 0, 1)

        yp, kp, vp, cp = _layer(yp, p_prompt[i].reshape(bp * t, -1), attend_p, conv_p, lw, dims)
        ys, ksn, vsn, csn = _layer(ys, p_sample[i].reshape(bs, -1), attend_s, conv_s, lw, dims)
        for lst, val in zip(outs, (kp.reshape(bp, t, n_heads, e), vp.reshape(bp, t, n_heads, e), cp,
                                   ksn.reshape(bs, 1, n_heads, e), vsn.reshape(bs, 1, n_heads, e), csn)):
            lst.append(val)
    stack = (lambda o: o[0][None]) if depth == 1 else jnp.stack
    return (yp.reshape(bp, t, d), ys.reshape(bs, 1, d), *[stack(o) for o in outs])
```

```python
import functools
import math

import jax
import jax.numpy as jnp
from jax import lax
from jax.experimental import pallas as pl
from jax.experimental.pallas import tpu as pltpu

F32 = jnp.float32
BF16 = jnp.bfloat16

EPS = 1e-6
NEG_INF = -1e30

LANES = 128
SUBLANES = 8
VMEM_BYTES_V7X = 64 * 1024 * 1024
VMEM_CEILING = VMEM_BYTES_V7X - 8 * 1024 * 1024
VMEM_INTERNAL = 12 * 1024 * 1024

ROW_TILE = 1024
COL_TILE = 512
NORM_ROWS = 256
ATTN_TILE = 512
CONV_ROWS = 128

LOG2E = 1.4426950408889634
BF16_EXACT_INT = 256
BF16_PIECES = 3


def _tile(n, pref):
    if n <= pref:
        return n
    if pref < LANES:
        assert n % pref == 0, (n, pref)
        return pref
    t = (pref // LANES) * LANES
    while t > LANES and n % t:
        t -= LANES
    assert n % t == 0, (n, pref)
    return t


def _nbytes(shape, dtype):
    return math.prod(shape) * jnp.dtype(dtype).itemsize


def _params(semantics, blocks, scratch=()):
    need = 2 * sum(_nbytes(s, d) for s, d in blocks)
    need += sum(_nbytes(s, d) for s, d in scratch) + VMEM_INTERNAL
    return pltpu.CompilerParams(dimension_semantics=semantics,
                                vmem_limit_bytes=min(need, VMEM_CEILING))


def _dot(a, b):
    return jnp.dot(a, b, preferred_element_type=F32)


def _dot_nt(a, b):
    return lax.dot_general(a, b, (((1,), (1,)), ((), ())), preferred_element_type=F32)


def _rms(x, g):
    return x * lax.rsqrt(jnp.mean(x * x, axis=-1, keepdims=True) + EPS) * g


def _lambda(lq1, lk1, lq2, lk2, lam_init):
    s1 = jnp.sum(lq1[...] * lk1[...], axis=-1, keepdims=True)
    s2 = jnp.sum(lq2[...] * lk2[...], axis=-1, keepdims=True)
    return jnp.exp(s1) - jnp.exp(s2) + lam_init


def _in_proj_kernel(x_ref, g_ref, wa_ref, wb_ref, h_ref, q_ref, k_ref, v_ref, u_ref, *, scale, n_attn):
    j = pl.program_id(1)

    @pl.when(j == 0)
    def _():
        rows = _tile(x_ref.shape[0], NORM_ROWS)
        for r in range(0, x_ref.shape[0], rows):
            h_ref[pl.ds(r, rows), :] = _rms(x_ref[pl.ds(r, rows), :], g_ref[...]).astype(BF16)

    @pl.when(j < n_attn)
    def _():
        q_ref[...] = (_dot(h_ref[...], wa_ref[...]) * scale).astype(BF16)

    @pl.when((j >= n_attn) & (j < 2 * n_attn))
    def _():
        k_ref[...] = _dot(h_ref[...], wa_ref[...])

    @pl.when((j >= 2 * n_attn) & (j < 3 * n_attn))
    def _():
        v_ref[...] = _dot(h_ref[...], wa_ref[...])

    @pl.when(j >= 3 * n_attn)
    def _():
        h = h_ref[...]
        u_ref[...] = _dot(h, wa_ref[...]) * jax.nn.sigmoid(_dot(h, wb_ref[...]))


def _merge_kernel(h_ref, o_ref, c_ref, wga_ref, wgc_ref, wa_ref, wc_ref, m_ref):
    h = h_ref[...]
    a = _dot(o_ref[...], wa_ref[...])
    c = _dot(c_ref[...], wc_ref[...])
    m = jax.nn.sigmoid(_dot(h, wga_ref[...])) * a + jax.nn.sigmoid(_dot(h, wgc_ref[...])) * c
    m_ref[...] = m.astype(BF16)


def _out_kernel(m_ref, w_ref, x_ref, gpost_ref, gpre_ref, x1_ref, f_ref):
    mix = _dot(m_ref[...], w_ref[...])
    x1 = x_ref[...] + _rms(mix, gpost_ref[...])
    x1_ref[...] = x1
    f_ref[...] = _rms(x1, gpre_ref[...]).astype(BF16)


def _ffn_up_kernel(f_ref, wg_ref, wu_ref, hd_ref):
    f = f_ref[...]
    hd_ref[...] = (jax.nn.silu(_dot(f, wg_ref[...])) * _dot(f, wu_ref[...])).astype(BF16)


def _ffn_down_kernel(hd_ref, w_ref, x1_ref, g_ref, x2_ref, y_ref):
    j = pl.program_id(1)
    tn = w_ref.shape[1]
    y_ref[:, pl.ds(pl.multiple_of(j * tn, tn), tn)] = _dot(hd_ref[...], w_ref[...])

    @pl.when(j == pl.num_programs(1) - 1)
    def _():
        rows = _tile(y_ref.shape[0], NORM_ROWS)
        for r in range(0, y_ref.shape[0], rows):
            sl = pl.ds(r, rows)
            x2_ref[sl, :] = x1_ref[sl, :] + _rms(y_ref[sl, :], g_ref[...])


def _ple_kernel(x2_ref, pe_ref, wg_ref, wp_ref, y_ref, x2b_ref):
    j = pl.program_id(1)
    tn = y_ref.shape[1]

    @pl.when(j == 0)
    def _():
        x2b_ref[...] = x2_ref[...].astype(BF16)

    gate = jax.nn.sigmoid(_dot(x2b_ref[...], wg_ref[...]))
    res = x2_ref[:, pl.ds(pl.multiple_of(j * tn, tn), tn)]
    y_ref[...] = res + gate * _dot(pe_ref[...].astype(BF16), wp_ref[...])


def _bf16_pieces(x):
    pieces = []
    for _ in range(BF16_PIECES):
        p = x.astype(BF16).astype(F32)
        pieces.append(p)
        x = x - p
    return pieces


def _alibi_lanes(lane0, lane, groups):
    out = jnp.zeros(lane.shape, F32)
    for g, pieces in enumerate(groups):
        for i, p in enumerate(pieces):
            out = jnp.where(lane == lane0 + g * BF16_PIECES + i, p, out)
    return out


def _attn_kernel(pt_ref, slopes_ref, lq1, lk1, lq2, lk2, gsub_ref, kx1_ref, kx2_ref, mask_ref, q_ref, k_ref,
                 v_ref, sig_ref, khead_ref, kpos_ref, qd_ref, kn_ref, vn_ref, *refs, lam_init, dh, t, group):
    del pt_ref
    kc_refs, vc_refs = refs[:group], refs[group:2 * group]
    (o_ref, od_ref, k1_ref, k2_ref, vt_ref, sa_ref, sb_ref, p_ref, acc_ref,
     qs_ref, bias_ref, dm_ref, dl_ref, dacc_ref) = refs[2 * group:]
    seq, e = k_ref.shape[1], k_ref.shape[2]
    n_blocks = seq // t
    qi = pl.program_id(2)
    sigma = slopes_ref[pl.program_id(1)] * LOG2E
    n_heads = qd_ref.shape[1]
    cols = kc_refs[0].shape[1]
    ps = cols // n_heads
    dsig = sig_ref[...]

    @pl.when(qi == 0)
    def _():
        k = k_ref[0].astype(BF16)
        lane = lax.broadcasted_iota(jnp.int32, k.shape, 1)
        k1_ref[...] = jnp.where(lane < dh, k, kx1_ref[...])
        k2_ref[...] = jnp.where(lane >= dh, k, kx2_ref[...])
        vt_ref[...] = v_ref[0].T.astype(BF16)
        qd = qd_ref[0]
        lane = lax.broadcasted_iota(jnp.int32, qd.shape, 1)
        zero = jnp.zeros_like(qd)
        qs_ref[...] = jnp.concatenate([jnp.where(lane < dh, qd, zero), jnp.where(lane >= dh, qd, zero)], axis=0)
        row_head = lax.broadcasted_iota(jnp.int32, bias_ref.shape, 0) % n_heads
        bias_ref[...] = jnp.where(khead_ref[...] == row_head, dsig * kpos_ref[...], NEG_INF)
        dm_ref[...] = jnp.full_like(dm_ref, NEG_INF)
        dl_ref[...] = jnp.zeros_like(dl_ref)
        dacc_ref[...] = jnp.zeros_like(dacc_ref)

    lam = _lambda(lq1, lk1, lq2, lk2, lam_init)

    def queries(c):
        q = q_ref[0, pl.ds(c * t, t), :].astype(F32)
        lane = lax.broadcasted_iota(jnp.int32, (t, e), 1)
        qpos = (c * t + lax.broadcasted_iota(jnp.int32, (t, e), 0)).astype(F32)
        sig = jnp.full((t, e), sigma, F32)
        sig_p = _bf16_pieces(sig)
        groups = [sig_p, sig_p, _bf16_pieces(-sig * qpos)]
        return (jnp.where(lane < dh, q, _alibi_lanes(dh, lane, groups)).astype(BF16),
                jnp.where(lane >= dh, q, _alibi_lanes(0, lane, groups)).astype(BF16))

    def scores(s_ref, qs, kj):
        keys = pl.ds(kj * t, t)
        s_ref[:, pl.ds(0, t)] = _dot_nt(k1_ref[keys, :], qs[0])
        s_ref[:, pl.ds(t, t)] = _dot_nt(k2_ref[keys, :], qs[1])

    def update(s_ref, kj, m, l, first, masked):
        m_out, l_out, alphas = [], [], []
        for c in range(2 * t // LANES):
            lanes = pl.ds(c * LANES, LANES)
            sc = s_ref[:, lanes]
            if masked:
                sc = sc + mask_ref[:, lanes]
            mc = jnp.max(sc, axis=0, keepdims=True)
            if not first:
                mc = jnp.maximum(m[c], mc)
            p = jnp.exp2(sc - mc)
            lc = jnp.sum(p, axis=0, keepdims=True)
            if not first:
                alpha = jnp.exp2(m[c] - mc)
                lc = alpha * l[c] + lc
                alphas.append(alpha)
            p_ref[:, lanes] = p.astype(BF16)
            m_out.append(mc)
            l_out.append(lc)
        pv = _dot(vt_ref[:, pl.ds(kj * t, t)], p_ref[...])
        if first:
            acc_ref[...] = pv
        else:
            acc_ref[...] = jnp.concatenate(alphas, axis=1) * acc_ref[...] + pv
        return m_out, l_out

    def prompt_block(c):
        score_refs = (sa_ref, sb_ref)
        qs = queries(c)
        scores(score_refs[0], qs, 0)
        m = l = None
        for kj in range(c + 1):
            if kj < c:
                scores(score_refs[(kj + 1) % 2], qs, kj + 1)
            m, l = update(score_refs[kj % 2], kj, m, l, first=(kj == 0), masked=(kj == c))
        l_row = jnp.concatenate(l, axis=1)
        acc = acc_ref[...]
        o = (acc[:, :t] / l_row[:, :t] - lam * (acc[:, t:] / l_row[:, t:])).T
        o_ref[0, pl.ds(c * t, t), :] = (_rms(o, gsub_ref[...]) * (1.0 - lam_init)).astype(BF16)

    def sample_pages(c):
        shift = -dsig * float((n_blocks - c) * group * ps)
        qs = qs_ref[...]
        s = jnp.concatenate([_dot_nt(qs, kc[0].astype(BF16)) for kc in kc_refs], axis=1) + bias_ref[...]
        m_old = dm_ref[...]
        m_new = jnp.maximum(m_old, jnp.max(s, axis=-1, keepdims=True) + shift)
        p = jnp.exp2(s - (m_new - shift)).astype(BF16)
        alpha = jnp.exp2(m_old - m_new)
        dm_ref[...] = m_new
        dl_ref[...] = alpha * dl_ref[...] + jnp.sum(p.astype(F32), axis=-1, keepdims=True)
        pv = _dot(p[:, :cols], vc_refs[0][0].astype(BF16))
        for g in range(1, group):
            pv = pv + _dot(p[:, g * cols:(g + 1) * cols], vc_refs[g][0].astype(BF16))
        dacc_ref[...] = alpha * dacc_ref[...] + pv

    for c in range(n_blocks):
        @pl.when(qi == c)
        def _(c=c):
            sample_pages(c)
            prompt_block(c)

    @pl.when(qi == n_blocks - 1)
    def _():
        prod = qd_ref[0].astype(F32) * kn_ref[0].astype(BF16).astype(F32)
        lane = lax.broadcasted_iota(jnp.int32, prod.shape, 1)
        s1 = jnp.sum(jnp.where(lane < dh, prod, 0.0), axis=-1, keepdims=True)
        s2 = jnp.sum(jnp.where(lane >= dh, prod, 0.0), axis=-1, keepdims=True)
        s_self = jnp.concatenate([s1, s2], axis=0)
        vn = vn_ref[0].astype(BF16).astype(F32)
        m_old = dm_ref[...]
        m_fin = jnp.maximum(m_old, s_self)
        alpha = jnp.exp2(m_old - m_fin)
        p_self = jnp.exp2(s_self - m_fin)
        l = alpha * dl_ref[...] + p_self
        acc = alpha * dacc_ref[...] + p_self * jnp.concatenate([vn, vn], axis=0)
        a = acc / l
        o = a[:n_heads] - lam * a[n_heads:]
        od_ref[0] = (_rms(o, gsub_ref[...]) * (1.0 - lam_init)).astype(BF16)


def _ln_silu(c, g, b):
    mu = jnp.mean(c, axis=-1, keepdims=True)
    var = jnp.mean(jnp.square(c - mu), axis=-1, keepdims=True)
    y = (c - mu) * lax.rsqrt(var + EPS) * g + b
    return jax.nn.silu(y)


def _prompt_conv_kernel(prev_ref, cur_ref, w_ref, cb_ref, g_ref, b_ref, o_ref, full_ref, c_ref, sh_ref):
    hist = prev_ref.shape[1]
    tt, ch = c_ref.shape
    width = w_ref.shape[0]

    @pl.when(pl.program_id(1) == 0)
    def _():
        full_ref[pl.ds(0, hist), :] = jnp.zeros((hist, ch), F32)

    @pl.when(pl.program_id(1) > 0)
    def _():
        full_ref[pl.ds(0, hist), :] = prev_ref[0]

    full_ref[pl.ds(hist, tt), :] = cur_ref[0]

    base = hist - (width - 1)

    def chunk(cc, _):
        lanes = pl.ds(pl.multiple_of(cc * LANES, LANES), LANES)
        acc = jnp.broadcast_to(cb_ref[:, lanes], (tt, LANES))
        for r in range(min(SUBLANES, width)):
            taps = range(r, width, SUBLANES)
            span = tt + taps[-1] - r
            sh_ref[r, pl.ds(0, span), :] = full_ref[pl.ds(base + r, span), lanes]
            for k in taps:
                acc = acc + w_ref[pl.ds(k, 1), lanes] * sh_ref[r, pl.ds(k - r, tt), :]
        c_ref[:, lanes] = acc
        return 0

    lax.fori_loop(0, ch // LANES, chunk, 0)
    o_ref[0] = _ln_silu(c_ref[...], g_ref[...], b_ref[...]).astype(BF16)


def _sample_conv_kernel(state_ref, u_ref, w_ref, cb_ref, g_ref, b_ref, o_ref, ns_ref):
    hist = state_ref.shape[0]
    u = u_ref[...]
    c = u * w_ref[pl.ds(hist, 1), :] + cb_ref[...]
    for k in range(hist):
        c = c + state_ref[k] * w_ref[pl.ds(k, 1), :]
    o_ref[...] = _ln_silu(c, g_ref[...], b_ref[...]).astype(BF16)
    for k in range(hist - 1):
        ns_ref[k] = state_ref[k + 1]
    ns_ref[hist - 1] = u


def _row_spec(tm, n):
    return pl.BlockSpec((tm, n), lambda i, j: (i, 0))


def _col_spec(k, tn, off_blocks=0):
    return pl.BlockSpec((k, tn), lambda i, j: (0, j + off_blocks))


def _tile_spec(tm, tn):
    return pl.BlockSpec((tm, tn), lambda i, j: (i, j))


def _vec_spec(n):
    return pl.BlockSpec((1, n), lambda i, j: (0, 0))


def _in_proj(x, g, w_in, aw, cd, scale):
    m, d = x.shape
    tm, tn = _tile(m, ROW_TILE), _tile(math.gcd(aw, cd), COL_TILE)
    na, nu = aw // tn, cd // tn
    gate0 = (3 * aw + cd) // tn

    def out_tile(first, count):
        return pl.BlockSpec((tm, tn), lambda i, j: (i, jnp.clip(j - first, 0, count - 1)))

    blocks = [((tm, d), F32), ((1, d), F32), ((d, tn), BF16), ((d, tn), BF16), ((tm, d), BF16),
              ((tm, tn), BF16), ((tm, tn), F32), ((tm, tn), F32), ((tm, tn), F32)]
    return pl.pallas_call(
        functools.partial(_in_proj_kernel, scale=scale, n_attn=na),
        grid=(m // tm, 3 * na + nu),
        in_specs=[_row_spec(tm, d), _vec_spec(d), _col_spec(d, tn),
                  pl.BlockSpec((d, tn), lambda i, j: (0, gate0 + jnp.maximum(j - 3 * na, 0)))],
        out_specs=[_row_spec(tm, d), out_tile(0, na), out_tile(na, na), out_tile(2 * na, na),
                   out_tile(3 * na, nu)],
        out_shape=[jax.ShapeDtypeStruct((m, d), BF16), jax.ShapeDtypeStruct((m, aw), BF16),
                   jax.ShapeDtypeStruct((m, aw), F32), jax.ShapeDtypeStruct((m, aw), F32),
                   jax.ShapeDtypeStruct((m, cd), F32)],
        compiler_params=_params(("parallel", "arbitrary"), blocks),
        name="in_proj",
    )(x, g, w_in, w_in)


def _merge(h, o, c, w_in, off, w_attn_out, w_conv_out):
    m, d = h.shape
    aw, cd = o.shape[1], c.shape[1]
    tm, tn = _tile(m, ROW_TILE), _tile(d, COL_TILE // 2)
    assert off % tn == 0
    blocks = [((tm, d), BF16), ((tm, aw), BF16), ((tm, cd), BF16), ((d, tn), BF16), ((d, tn), BF16),
              ((aw, tn), BF16), ((cd, tn), BF16), ((tm, tn), BF16)]
    return pl.pallas_call(
        _merge_kernel,
        grid=(m // tm, d // tn),
        in_specs=[_row_spec(tm, d), _row_spec(tm, aw), _row_spec(tm, cd),
                  _col_spec(d, tn, off // tn), _col_spec(d, tn, (off + d) // tn),
                  _col_spec(aw, tn), _col_spec(cd, tn)],
        out_specs=_tile_spec(tm, tn),
        out_shape=jax.ShapeDtypeStruct((m, d), BF16),
        compiler_params=_params(("parallel", "arbitrary"), blocks),
        name="gated_merge",
    )(h, o, c, w_in, w_in, w_attn_out, w_conv_out)


def _out_proj(mix_in, w_out, x, g_post, g_pre):
    m, d = x.shape
    tm = _tile(m, ROW_TILE // 4)
    row = pl.BlockSpec((tm, d), lambda i: (i, 0))
    vec = pl.BlockSpec((1, d), lambda i: (0, 0))
    blocks = [((tm, d), BF16), ((d, d), BF16), ((tm, d), F32), ((tm, d), F32), ((tm, d), BF16)]
    return pl.pallas_call(
        _out_kernel,
        grid=(m // tm,),
        in_specs=[row, pl.BlockSpec((d, d), lambda i: (0, 0)), row, vec, vec],
        out_specs=[row, row],
        out_shape=[jax.ShapeDtypeStruct((m, d), F32), jax.ShapeDtypeStruct((m, d), BF16)],
        compiler_params=_params(("parallel",), blocks),
        name="out_proj",
    )(mix_in, w_out, x, g_post, g_pre)


def _ffn_up(f, w_gate, w_up):
    m, d = f.shape
    dff = w_gate.shape[1]
    tm, tn = _tile(m, ROW_TILE), _tile(dff, COL_TILE)
    blocks = [((tm, d), BF16), ((d, tn), BF16), ((d, tn), BF16), ((tm, tn), BF16)]
    return pl.pallas_call(
        _ffn_up_kernel,
        grid=(m // tm, dff // tn),
        in_specs=[_row_spec(tm, d), _col_spec(d, tn), _col_spec(d, tn)],
        out_specs=_tile_spec(tm, tn),
        out_shape=jax.ShapeDtypeStruct((m, dff), BF16),
        compiler_params=_params(("parallel", "arbitrary"), blocks),
        name="ffn_up",
    )(f, w_gate, w_up)


def _ffn_down(hd, w_down, x1, g_post):
    m, dff = hd.shape
    d = x1.shape[1]
    tm, tn = _tile(m, ROW_TILE // 2), _tile(d, COL_TILE)
    row = pl.BlockSpec((tm, d), lambda i, j: (i, 0))
    blocks = [((tm, dff), BF16), ((dff, tn), BF16), ((tm, d), F32), ((tm, d), F32)]
    scratch = [((tm, d), F32)]
    return pl.pallas_call(
        _ffn_down_kernel,
        grid=(m // tm, d // tn),
        in_specs=[_row_spec(tm, dff), _col_spec(dff, tn), row, _vec_spec(d)],
        out_specs=row,
        out_shape=jax.ShapeDtypeStruct((m, d), F32),
        scratch_shapes=[pltpu.VMEM(s, dt) for s, dt in scratch],
        compiler_params=_params(("parallel", "arbitrary"), blocks, scratch),
        name="ffn_down",
    )(hd, w_down, x1, g_post)


def _ple(x2, pe, w_gate, w_proj):
    m, d = x2.shape
    pd = pe.shape[1]
    tm, tn = _tile(m, ROW_TILE), _tile(d, COL_TILE)
    blocks = [((tm, d), F32), ((tm, pd), F32), ((d, tn), BF16), ((pd, tn), BF16), ((tm, tn), F32)]
    scratch = [((tm, d), BF16)]
    return pl.pallas_call(
        _ple_kernel,
        grid=(m // tm, d // tn),
        in_specs=[_row_spec(tm, d), _row_spec(tm, pd), _col_spec(d, tn), _col_spec(pd, tn)],
        out_specs=_tile_spec(tm, tn),
        out_shape=jax.ShapeDtypeStruct((m, d), F32),
        scratch_shapes=[pltpu.VMEM(s, dt) for s, dt in scratch],
        compiler_params=_params(("parallel", "arbitrary"), blocks, scratch),
        name="ple",
    )(x2, pe, w_gate, w_proj)


def _lambda_specs(dh, index_map):
    return [pl.BlockSpec((1, dh), index_map)] * 4


def _alibi_key_lanes(t, e, lane0):
    pos = jnp.arange(t, dtype=jnp.int32)[:, None]
    lo = pos % BF16_EXACT_INT
    g = (jnp.arange(e, dtype=jnp.int32)[None, :] - lane0) // BF16_PIECES
    vals = jnp.where(g == 0, pos - lo, jnp.where(g == 1, lo, 1))
    return jnp.where((g >= 0) & (g < 3), vals, 0).astype(BF16)


def _attention(q, k, v, qd, kd, vd, cache_k, cache_v, page_table, slopes, lams, g_subln, lam_init, n_heads):
    b, t, aw = q.shape
    bs = qd.shape[0]
    e = aw // n_heads
    dh = e // 2
    assert e == LANES and dh >= 3 * BF16_PIECES, "ALiBi lanes ride in the unused half of each map's copy"
    assert bs == b * n_heads, "one sample sequence rides on each (batch, head) of the prompt grid"
    tq = _tile(t, ATTN_TILE)
    nq = t // tq
    n_pool, ps = cache_k.shape[:2]
    n_pages = page_table.shape[1]
    assert n_pages % nq == 0, "each query-block step takes an equal share of the pages"
    group = n_pages // nq
    rows, cols = 2 * n_heads, ps * n_heads
    key = jnp.arange(tq, dtype=jnp.int32)[:, None]
    qry = jnp.arange(2 * tq, dtype=jnp.int32)[None, :] % tq
    mask = jnp.where(key <= qry, 0.0, NEG_INF).astype(F32)
    col = jnp.arange(group * cols, dtype=jnp.int32)[None, :]

    const = lambda bi, h, qi, pt: (0, 0)
    seq = pl.BlockSpec((1, t, e), lambda bi, h, qi, pt: (bi, 0, h))
    tok = pl.BlockSpec((1, n_heads, e), lambda bi, h, qi, pt: (bi * n_heads + h, 0, 0))
    pages = [pl.BlockSpec((1, cols, e),
                          lambda bi, h, qi, pt, g=g: (pt[(bi * n_heads + h) * n_pages + qi * group + g], 0, 0))
             for g in range(group)]
    blocks = [((t, e), F32), ((t, e), F32), ((t, e), BF16), ((t, e), BF16), ((t, e), BF16), ((t, e), BF16),
              ((tq, 2 * tq), F32)] + [((cols, e), F32)] * (2 * group)
    scratch = [((t, e), BF16), ((t, e), BF16), ((e, t), BF16), ((tq, 2 * tq), F32), ((tq, 2 * tq), F32),
               ((tq, 2 * tq), BF16), ((e, 2 * tq), F32),
               ((rows, e), BF16), ((rows, group * cols), F32), ((rows, 1), F32), ((rows, 1), F32), ((rows, e), F32)]
    grid_spec = pltpu.PrefetchScalarGridSpec(
        num_scalar_prefetch=1,
        grid=(b, n_heads, nq),
        in_specs=[pl.BlockSpec(memory_space=pltpu.SMEM)] + _lambda_specs(dh, const)
        + [pl.BlockSpec((1, e), const), pl.BlockSpec((t, e), const), pl.BlockSpec((t, e), const),
           pl.BlockSpec((tq, 2 * tq), const), seq, seq, seq,
           pl.BlockSpec((rows, 1), const), pl.BlockSpec((1, group * cols), const),
           pl.BlockSpec((1, group * cols), const), tok, tok, tok] + pages + pages,
        out_specs=[seq, tok],
        scratch_shapes=[pltpu.VMEM(s, d) for s, d in scratch],
    )
    pool_k = cache_k.reshape(n_pool, cols, e)
    pool_v = cache_v.reshape(n_pool, cols, e)
    o, od = pl.pallas_call(
        functools.partial(_attn_kernel, lam_init=lam_init, dh=dh, t=tq, group=group),
        grid_spec=grid_spec,
        out_shape=[jax.ShapeDtypeStruct((b, t, aw), BF16), jax.ShapeDtypeStruct((bs, n_heads, e), BF16)],
        compiler_params=_params(("parallel", "parallel", "arbitrary"), blocks, scratch),
        name="attention",
    )(page_table.reshape(-1), slopes, *lams, g_subln, _alibi_key_lanes(t, e, dh), _alibi_key_lanes(t, e, 0), mask,
      q, k, v, (jnp.tile(slopes, 2) * LOG2E)[:, None], col % n_heads, (col // n_heads).astype(F32),
      qd.reshape(bs, n_heads, e), kd.reshape(bs, n_heads, e), vd.reshape(bs, n_heads, e),
      *([pool_k] * group), *([pool_v] * group))
    return o, od.reshape(bs, aw)


def _prompt_conv(u, conv_w, conv_b, g, b):
    bsz, t, ch = u.shape
    width = conv_w.shape[0]
    hist = -(-(width - 1) // SUBLANES) * SUBLANES
    tt = _tile(t, CONV_ROWS)
    assert tt % hist == 0
    const = lambda bi, ti: (0, 0)
    vec = pl.BlockSpec((1, ch), const)
    blocks = [((hist, ch), F32), ((tt, ch), F32), ((width, ch), F32), ((tt, ch), BF16)]
    scratch = [((hist + tt, ch), F32), ((tt, ch), F32), ((SUBLANES, hist + tt, LANES), F32)]
    return pl.pallas_call(
        _prompt_conv_kernel,
        grid=(bsz, t // tt),
        in_specs=[pl.BlockSpec((1, hist, ch), lambda bi, ti: (bi, jnp.maximum(ti * (tt // hist) - 1, 0), 0)),
                  pl.BlockSpec((1, tt, ch), lambda bi, ti: (bi, ti, 0)),
                  pl.BlockSpec((width, ch), const), vec, vec, vec],
        out_specs=pl.BlockSpec((1, tt, ch), lambda bi, ti: (bi, ti, 0)),
        out_shape=jax.ShapeDtypeStruct((bsz, t, ch), BF16),
        scratch_shapes=[pltpu.VMEM(s, d) for s, d in scratch],
        compiler_params=_params(("parallel", "arbitrary"), blocks, scratch),
        name="prompt_conv",
    )(u, u, conv_w, conv_b, g, b)


def _sample_conv(state, u, conv_w, conv_b, g, b):
    hist, bs, ch = state.shape
    width = conv_w.shape[0]
    assert hist == width - 1
    bt = _tile(bs, 4 * SUBLANES) if bs % SUBLANES == 0 else bs
    vec = pl.BlockSpec((1, ch), lambda i: (0, 0))
    st = pl.BlockSpec((hist, bt, ch), lambda i: (0, i, 0))
    blocks = [((hist, bt, ch), F32), ((bt, ch), F32), ((width, ch), F32), ((bt, ch), BF16), ((hist, bt, ch), F32)]
    return pl.pallas_call(
        _sample_conv_kernel,
        grid=(bs // bt,),
        in_specs=[st, pl.BlockSpec((bt, ch), lambda i: (i, 0)), pl.BlockSpec((width, ch), lambda i: (0, 0)),
                  vec, vec, vec],
        out_specs=[pl.BlockSpec((bt, ch), lambda i: (i, 0)), st],
        out_shape=[jax.ShapeDtypeStruct((bs, ch), BF16), jax.ShapeDtypeStruct((hist, bs, ch), F32)],
        compiler_params=_params(("parallel",), blocks),
        name="sample_conv",
    )(state, u, conv_w, conv_b, g, b)


def _layer_tail(x, pe, h, o, c, lw, dims):
    aw, cd = dims["attn_width"], dims["conv_dim"]
    mix_in = _merge(h, o, c, lw["w_in"], 3 * aw + 2 * cd, lw["w_attn_out"], lw["w_conv_out"])
    x1, f = _out_proj(mix_in, lw["w_out"], x, lw["g_post_mix"], lw["g_pre_ffn"])
    hd = _ffn_up(f, lw["w_ffn_gate"], lw["w_ffn_up"])
    x2 = _ffn_down(hd, lw["w_ffn_down"], x1, lw["g_post_ffn"])
    return _ple(x2, pe, lw["w_ple_gate"], lw["w_ple_proj"])


_MATRICES = ("w_in", "w_attn_out", "w_conv_out", "w_out", "w_ffn_gate", "w_ffn_up", "w_ffn_down",
             "w_ple_proj", "w_ple_gate")
_VECTORS = ("g_pre_mix", "g_subln", "conv_b", "g_conv_norm", "b_conv_norm", "g_post_mix", "g_pre_ffn",
            "g_post_ffn", "lambda_q1", "lambda_k1", "lambda_q2", "lambda_k2")


def kernel(x_prompt, x_sample, cache_k, cache_v, state_conv, page_table, p_prompt, p_sample, g_pre_mix, w_in, lambda_q1, lambda_k1, lambda_q2, lambda_k2, g_subln, w_attn_out, conv_w, conv_b, g_conv_norm, b_conv_norm, w_conv_out, w_out, g_post_mix, g_pre_ffn, w_ffn_gate, w_ffn_up, w_ffn_down, g_post_ffn, w_ple_proj, w_ple_gate):
    weights = dict(g_pre_mix=g_pre_mix, w_in=w_in, lambda_q1=lambda_q1, lambda_k1=lambda_k1,
                   lambda_q2=lambda_q2, lambda_k2=lambda_k2, g_subln=g_subln, w_attn_out=w_attn_out,
                   conv_w=conv_w, conv_b=conv_b, g_conv_norm=g_conv_norm, b_conv_norm=b_conv_norm,
                   w_conv_out=w_conv_out, w_out=w_out, g_post_mix=g_post_mix, g_pre_ffn=g_pre_ffn,
                   w_ffn_gate=w_ffn_gate, w_ffn_up=w_ffn_up, w_ffn_down=w_ffn_down, g_post_ffn=g_post_ffn,
                   w_ple_proj=w_ple_proj, w_ple_gate=w_ple_gate)
    depth = w_in.shape[0]
    bp, t, d = x_prompt.shape
    bs, ts, _ = x_sample.shape
    assert ts == 1, "the sample group decodes one token per sequence"
    n_heads, e = cache_k.shape[3], cache_k.shape[4]
    dims = dict(attn_width=n_heads * e, conv_dim=conv_w.shape[2], attn_scale=(e // 2) ** -0.5 * LOG2E)
    slopes = 2.0 ** (-8.0 * jnp.arange(1, n_heads + 1, dtype=F32) / n_heads)

    yp = x_prompt.reshape(bp * t, d)
    ys = x_sample.reshape(bs, d)
    n_pool = cache_k.shape[1]
    pool_k = cache_k.reshape((depth * n_pool,) + cache_k.shape[2:])
    pool_v = cache_v.reshape((depth * n_pool,) + cache_v.shape[2:])
    outs = [[] for _ in range(6)]
    for i in range(depth):
        lam_init = 0.8 - 0.6 * math.exp(-0.3 * i)
        lw = {n: weights[n][i].astype(BF16) for n in _MATRICES}
        lw.update({n: weights[n][i][None, :] for n in _VECTORS})
        lw["conv_w"] = conv_w[i]
        lams = [lw[n] for n in ("lambda_q1", "lambda_k1", "lambda_q2", "lambda_k2")]
        ln = (lw["conv_b"], lw["g_conv_norm"], lw["b_conv_norm"])

        aw, cd = dims["attn_width"], dims["conv_dim"]
        hp, qp, kp, vp, up = _in_proj(yp, lw["g_pre_mix"], lw["w_in"], aw, cd, dims["attn_scale"])
        hs, qs, ksn, vsn, us = _in_proj(ys, lw["g_pre_mix"], lw["w_in"], aw, cd, dims["attn_scale"])
        shape = (bp, t, aw)
        op, os_ = _attention(qp.reshape(shape), kp.reshape(shape), vp.reshape(shape), qs, ksn, vsn, pool_k, pool_v,
                             page_table + i * n_pool, slopes, lams, lw["g_subln"], lam_init, n_heads)
        up3 = up.reshape(bp, t, cd)
        cp_act = _prompt_conv(up3, lw["conv_w"], *ln).reshape(bp * t, cd)
        cp = up3[:, t - (conv_w.shape[1] - 1):, :]
        cs_act, csn = _sample_conv(jnp.swapaxes(state_conv[i], 0, 1), us, lw["conv_w"], *ln)
        csn = jnp.swapaxes(csn, 0, 1)
        yp = _layer_tail(yp, p_prompt[i].reshape(bp * t, -1), hp, op.reshape(bp * t, aw), cp_act, lw, dims)
        ys = _layer_tail(ys, p_sample[i].reshape(bs, -1), hs, os_, cs_act, lw, dims)
        for lst, val in zip(outs, (kp.reshape(bp, t, n_heads, e), vp.reshape(bp, t, n_heads, e), cp,
                                   ksn.reshape(bs, 1, n_heads, e), vsn.reshape(bs, 1, n_heads, e), csn)):
            lst.append(val)
    stack = (lambda o: o[0][None]) if depth == 1 else jnp.stack
    return (yp.reshape(bp, t, d), ys.reshape(bs, 1, d), *[stack(o) for o in outs])
```

```python
import functools
import math

import jax
import jax.numpy as jnp
from jax import lax
from jax.experimental import pallas as pl
from jax.experimental.pallas import tpu as pltpu

F32 = jnp.float32
BF16 = jnp.bfloat16

EPS = 1e-6
NEG_INF = -1e30

LANES = 128
SUBLANES = 8
VMEM_BYTES_V7X = 64 * 1024 * 1024
VMEM_CEILING = VMEM_BYTES_V7X - 8 * 1024 * 1024
VMEM_INTERNAL = 12 * 1024 * 1024

ROW_TILE = 1024
COL_TILE = 512
NORM_ROWS = 256
ATTN_TILE = 512
CONV_ROWS = 128

LOG2E = 1.4426950408889634
BF16_EXACT_INT = 256
BF16_PIECES = 3


def _tile(n, pref):
    if n <= pref:
        return n
    if pref < LANES:
        assert n % pref == 0, (n, pref)
        return pref
    t = (pref // LANES) * LANES
    while t > LANES and n % t:
        t -= LANES
    assert n % t == 0, (n, pref)
    return t


def _nbytes(shape, dtype):
    return math.prod(shape) * jnp.dtype(dtype).itemsize


def _params(semantics, blocks, scratch=()):
    need = 2 * sum(_nbytes(s, d) for s, d in blocks)
    need += sum(_nbytes(s, d) for s, d in scratch) + VMEM_INTERNAL
    return pltpu.CompilerParams(dimension_semantics=semantics,
                                vmem_limit_bytes=min(need, VMEM_CEILING))


def _dot(a, b):
    return jnp.dot(a, b, preferred_element_type=F32)


def _dot_nt(a, b):
    return lax.dot_general(a, b, (((1,), (1,)), ((), ())), preferred_element_type=F32)


def _rms(x, g):
    return x * lax.rsqrt(jnp.mean(x * x, axis=-1, keepdims=True) + EPS) * g


def _lambda(lq1, lk1, lq2, lk2, lam_init):
    s1 = jnp.sum(lq1[...] * lk1[...], axis=-1, keepdims=True)
    s2 = jnp.sum(lq2[...] * lk2[...], axis=-1, keepdims=True)
    return jnp.exp(s1) - jnp.exp(s2) + lam_init


def _in_proj_kernel(x_ref, g_ref, wa_ref, wb_ref, h_ref, q_ref, k_ref, v_ref, u_ref, *, scale, n_attn):
    j = pl.program_id(1)

    @pl.when(j == 0)
    def _():
        rows = _tile(x_ref.shape[0], NORM_ROWS)
        for r in range(0, x_ref.shape[0], rows):
            h_ref[pl.ds(r, rows), :] = _rms(x_ref[pl.ds(r, rows), :], g_ref[...]).astype(BF16)

    @pl.when(j < n_attn)
    def _():
        q_ref[...] = (_dot(h_ref[...], wa_ref[...]) * scale).astype(BF16)

    @pl.when((j >= n_attn) & (j < 2 * n_attn))
    def _():
        k_ref[...] = _dot(h_ref[...], wa_ref[...])

    @pl.when((j >= 2 * n_attn) & (j < 3 * n_attn))
    def _():
        v_ref[...] = _dot(h_ref[...], wa_ref[...])

    @pl.when(j >= 3 * n_attn)
    def _():
        h = h_ref[...]
        u_ref[...] = _dot(h, wa_ref[...]) * jax.nn.sigmoid(_dot(h, wb_ref[...]))


def _merge_kernel(h_ref, o_ref, c_ref, wga_ref, wgc_ref, wa_ref, wc_ref, m_ref):
    h = h_ref[...]
    a = _dot(o_ref[...], wa_ref[...])
    c = _dot(c_ref[...], wc_ref[...])
    m = jax.nn.sigmoid(_dot(h, wga_ref[...])) * a + jax.nn.sigmoid(_dot(h, wgc_ref[...])) * c
    m_ref[...] = m.astype(BF16)


def _out_kernel(m_ref, w_ref, x_ref, gpost_ref, gpre_ref, x1_ref, f_ref):
    rows = _tile(m_ref.shape[0], NORM_ROWS)
    for r in range(0, m_ref.shape[0], rows):
        sl = pl.ds(r, rows)
        x1 = x_ref[sl, :] + _rms(_dot(m_ref[sl, :], w_ref[...]), gpost_ref[...])
        x1_ref[sl, :] = x1
        f_ref[sl, :] = _rms(x1, gpre_ref[...]).astype(BF16)


def _ffn_up_kernel(f_ref, wg_ref, wu_ref, hd_ref):
    f = f_ref[...]
    hd_ref[...] = (jax.nn.silu(_dot(f, wg_ref[...])) * _dot(f, wu_ref[...])).astype(BF16)


def _ffn_down_kernel(hd_ref, w_ref, x1_ref, g_ref, x2_ref, y_ref):
    j = pl.program_id(1)
    tn = w_ref.shape[1]
    y_ref[:, pl.ds(pl.multiple_of(j * tn, tn), tn)] = _dot(hd_ref[...], w_ref[...])

    @pl.when(j == pl.num_programs(1) - 1)
    def _():
        rows = _tile(y_ref.shape[0], NORM_ROWS)
        for r in range(0, y_ref.shape[0], rows):
            sl = pl.ds(r, rows)
            x2_ref[sl, :] = x1_ref[sl, :] + _rms(y_ref[sl, :], g_ref[...])


def _ple_kernel(x2_ref, pe_ref, wg_ref, wp_ref, y_ref, x2b_ref):
    j = pl.program_id(1)
    tn = y_ref.shape[1]

    @pl.when(j == 0)
    def _():
        x2b_ref[...] = x2_ref[...].astype(BF16)

    gate = jax.nn.sigmoid(_dot(x2b_ref[...], wg_ref[...]))
    res = x2_ref[:, pl.ds(pl.multiple_of(j * tn, tn), tn)]
    y_ref[...] = res + gate * _dot(pe_ref[...].astype(BF16), wp_ref[...])


def _bf16_pieces(x):
    pieces = []
    for _ in range(BF16_PIECES):
        p = x.astype(BF16).astype(F32)
        pieces.append(p)
        x = x - p
    return pieces


def _alibi_lanes(lane0, lane, groups):
    out = jnp.zeros(lane.shape, F32)
    for g, pieces in enumerate(groups):
        for i, p in enumerate(pieces):
            out = jnp.where(lane == lane0 + g * BF16_PIECES + i, p, out)
    return out


def _attn_kernel(pt_ref, slopes_ref, lq1, lk1, lq2, lk2, gsub_ref, kx1_ref, kx2_ref, mask_ref, q_ref, k_ref,
                 v_ref, sig_ref, khead_ref, kpos_ref, qd_ref, kn_ref, vn_ref, *refs, lam_init, dh, t, group):
    del pt_ref
    kc_refs, vc_refs = refs[:group], refs[group:2 * group]
    (o_ref, od_ref, k1_ref, k2_ref, vt_ref, sa_ref, sb_ref, p_ref, acc_ref,
     qs_ref, bias_ref, dm_ref, dl_ref, dacc_ref) = refs[2 * group:]
    seq, e = k_ref.shape[1], k_ref.shape[2]
    n_blocks = seq // t
    qi = pl.program_id(2)
    sigma = slopes_ref[pl.program_id(1)] * LOG2E
    n_heads = qd_ref.shape[1]
    cols = kc_refs[0].shape[1]
    ps = cols // n_heads
    dsig = sig_ref[...]

    @pl.when(qi == 0)
    def _():
        k = k_ref[0].astype(BF16)
        lane = lax.broadcasted_iota(jnp.int32, k.shape, 1)
        k1_ref[...] = jnp.where(lane < dh, k, kx1_ref[...])
        k2_ref[...] = jnp.where(lane >= dh, k, kx2_ref[...])
        vt_ref[...] = v_ref[0].T.astype(BF16)
        qd = qd_ref[0]
        lane = lax.broadcasted_iota(jnp.int32, qd.shape, 1)
        zero = jnp.zeros_like(qd)
        qs_ref[...] = jnp.concatenate([jnp.where(lane < dh, qd, zero), jnp.where(lane >= dh, qd, zero)], axis=0)
        row_head = lax.broadcasted_iota(jnp.int32, bias_ref.shape, 0) % n_heads
        bias_ref[...] = jnp.where(khead_ref[...] == row_head, dsig * kpos_ref[...], NEG_INF)
        dm_ref[...] = jnp.full_like(dm_ref, NEG_INF)
        dl_ref[...] = jnp.zeros_like(dl_ref)
        dacc_ref[...] = jnp.zeros_like(dacc_ref)

    lam = _lambda(lq1, lk1, lq2, lk2, lam_init)

    def queries(c):
        q = q_ref[0, pl.ds(c * t, t), :].astype(F32)
        lane = lax.broadcasted_iota(jnp.int32, (t, e), 1)
        qpos = (c * t + lax.broadcasted_iota(jnp.int32, (t, e), 0)).astype(F32)
        sig = jnp.full((t, e), sigma, F32)
        sig_p = _bf16_pieces(sig)
        groups = [sig_p, sig_p, _bf16_pieces(-sig * qpos)]
        return (jnp.where(lane < dh, q, _alibi_lanes(dh, lane, groups)).T.astype(BF16),
                jnp.where(lane >= dh, q, _alibi_lanes(0, lane, groups)).T.astype(BF16))

    def scores(s_ref, qs, kj):
        keys = pl.ds(kj * t, t)
        s_ref[:, pl.ds(0, t)] = _dot(k1_ref[keys, :], qs[0])
        s_ref[:, pl.ds(t, t)] = _dot(k2_ref[keys, :], qs[1])

    def update(s_ref, kj, m, l, first, masked):
        m_out, l_out, alphas = [], [], []
        for c in range(2 * t // LANES):
            lanes = pl.ds(c * LANES, LANES)
            sc = s_ref[:, lanes]
            if masked:
                sc = sc + mask_ref[:, lanes]
            mc = jnp.max(sc, axis=0, keepdims=True)
            if not first:
                mc = jnp.maximum(m[c], mc)
            p = jnp.exp2(sc - mc)
            lc = jnp.sum(p, axis=0, keepdims=True)
            if not first:
                alpha = jnp.exp2(m[c] - mc)
                lc = alpha * l[c] + lc
                alphas.append(alpha)
            p_ref[:, lanes] = p.astype(BF16)
            m_out.append(mc)
            l_out.append(lc)
        pv = _dot(vt_ref[:, pl.ds(kj * t, t)], p_ref[...])
        if first:
            acc_ref[...] = pv
        else:
            acc_ref[...] = jnp.concatenate(alphas, axis=1) * acc_ref[...] + pv
        return m_out, l_out

    def prompt_block(c):
        score_refs = (sa_ref, sb_ref)
        qs = queries(c)
        scores(score_refs[0], qs, 0)
        m = l = None
        for kj in range(c + 1):
            if kj < c:
                scores(score_refs[(kj + 1) % 2], qs, kj + 1)
            m, l = update(score_refs[kj % 2], kj, m, l, first=(kj == 0), masked=(kj == c))
        l_row = jnp.concatenate(l, axis=1)
        acc = acc_ref[...]
        o = (acc[:, :t] / l_row[:, :t] - lam * (acc[:, t:] / l_row[:, t:])).T
        o_ref[0, pl.ds(c * t, t), :] = (_rms(o, gsub_ref[...]) * (1.0 - lam_init)).astype(BF16)

    def sample_pages(c):
        shift = -dsig * float((n_blocks - c) * group * ps)
        qs = qs_ref[...]
        s = jnp.concatenate([_dot_nt(qs, kc[0].astype(BF16)) for kc in kc_refs], axis=1) + bias_ref[...]
        m_old = dm_ref[...]
        m_new = jnp.maximum(m_old, jnp.max(s, axis=-1, keepdims=True) + shift)
        p = jnp.exp2(s - (m_new - shift))
        alpha = jnp.exp2(m_old - m_new)
        dm_ref[...] = m_new
        dl_ref[...] = alpha * dl_ref[...] + jnp.sum(p, axis=-1, keepdims=True)
        p = p.astype(BF16)
        pv = _dot(p[:, :cols], vc_refs[0][0].astype(BF16))
        for g in range(1, group):
            pv = pv + _dot(p[:, g * cols:(g + 1) * cols], vc_refs[g][0].astype(BF16))
        dacc_ref[...] = alpha * dacc_ref[...] + pv

    for c in range(n_blocks):
        @pl.when(qi == c)
        def _(c=c):
            sample_pages(c)
            prompt_block(c)

    @pl.when(qi == n_blocks - 1)
    def _():
        prod = qd_ref[0].astype(F32) * kn_ref[0].astype(BF16).astype(F32)
        lane = lax.broadcasted_iota(jnp.int32, prod.shape, 1)
        s1 = jnp.sum(jnp.where(lane < dh, prod, 0.0), axis=-1, keepdims=True)
        s2 = jnp.sum(jnp.where(lane >= dh, prod, 0.0), axis=-1, keepdims=True)
        s_self = jnp.concatenate([s1, s2], axis=0)
        vn = vn_ref[0].astype(BF16).astype(F32)
        m_old = dm_ref[...]
        m_fin = jnp.maximum(m_old, s_self)
        alpha = jnp.exp2(m_old - m_fin)
        p_self = jnp.exp2(s_self - m_fin)
        l = alpha * dl_ref[...] + p_self
        acc = alpha * dacc_ref[...] + p_self * jnp.concatenate([vn, vn], axis=0)
        a = acc / l
        o = a[:n_heads] - lam * a[n_heads:]
        od_ref[0] = (_rms(o, gsub_ref[...]) * (1.0 - lam_init)).astype(BF16)


def _ln_silu(c, g, b):
    mu = jnp.mean(c, axis=-1, keepdims=True)
    var = jnp.mean(jnp.square(c - mu), axis=-1, keepdims=True)
    y = (c - mu) * lax.rsqrt(var + EPS) * g + b
    return jax.nn.silu(y)


def _prompt_conv_kernel(prev_ref, cur_ref, w_ref, cb_ref, g_ref, b_ref, o_ref, full_ref, c_ref, sh_ref):
    hist = prev_ref.shape[1]
    tt, ch = c_ref.shape
    width = w_ref.shape[0]

    @pl.when(pl.program_id(1) == 0)
    def _():
        full_ref[pl.ds(0, hist), :] = jnp.zeros((hist, ch), F32)

    @pl.when(pl.program_id(1) > 0)
    def _():
        full_ref[pl.ds(0, hist), :] = prev_ref[0]

    full_ref[pl.ds(hist, tt), :] = cur_ref[0]

    base = hist - (width - 1)

    def chunk(cc, _):
        lanes = pl.ds(pl.multiple_of(cc * LANES, LANES), LANES)
        acc = jnp.broadcast_to(cb_ref[:, lanes], (tt, LANES))
        for r in range(min(SUBLANES, width)):
            taps = range(r, width, SUBLANES)
            span = tt + taps[-1] - r
            sh_ref[r, pl.ds(0, span), :] = full_ref[pl.ds(base + r, span), lanes]
            for k in taps:
                acc = acc + w_ref[pl.ds(k, 1), lanes] * sh_ref[r, pl.ds(k - r, tt), :]
        c_ref[:, lanes] = acc
        return 0

    lax.fori_loop(0, ch // LANES, chunk, 0)
    o_ref[0] = _ln_silu(c_ref[...], g_ref[...], b_ref[...]).astype(BF16)


def _sample_conv_kernel(state_ref, u_ref, w_ref, cb_ref, g_ref, b_ref, o_ref, ns_ref):
    hist = state_ref.shape[0]
    u = u_ref[...]
    c = u * w_ref[pl.ds(hist, 1), :] + cb_ref[...]
    for k in range(hist):
        c = c + state_ref[k] * w_ref[pl.ds(k, 1), :]
    o_ref[...] = _ln_silu(c, g_ref[...], b_ref[...]).astype(BF16)
    for k in range(hist - 1):
        ns_ref[k] = state_ref[k + 1]
    ns_ref[hist - 1] = u


def _row_spec(tm, n):
    return pl.BlockSpec((tm, n), lambda i, j: (i, 0))


def _col_spec(k, tn, off_blocks=0):
    return pl.BlockSpec((k, tn), lambda i, j: (0, j + off_blocks))


def _tile_spec(tm, tn):
    return pl.BlockSpec((tm, tn), lambda i, j: (i, j))


def _vec_spec(n):
    return pl.BlockSpec((1, n), lambda i, j: (0, 0))


def _in_proj(x, g, w_in, aw, cd, scale):
    m, d = x.shape
    tm, tn = _tile(m, ROW_TILE), _tile(math.gcd(aw, cd), COL_TILE)
    na, nu = aw // tn, cd // tn
    gate0 = (3 * aw + cd) // tn

    def out_tile(first, count):
        return pl.BlockSpec((tm, tn), lambda i, j: (i, jnp.clip(j - first, 0, count - 1)))

    blocks = [((tm, d), F32), ((1, d), F32), ((d, tn), BF16), ((d, tn), BF16), ((tm, d), BF16),
              ((tm, tn), BF16), ((tm, tn), F32), ((tm, tn), F32), ((tm, tn), F32)]
    return pl.pallas_call(
        functools.partial(_in_proj_kernel, scale=scale, n_attn=na),
        grid=(m // tm, 3 * na + nu),
        in_specs=[_row_spec(tm, d), _vec_spec(d), _col_spec(d, tn),
                  pl.BlockSpec((d, tn), lambda i, j: (0, gate0 + jnp.maximum(j - 3 * na, 0)))],
        out_specs=[_row_spec(tm, d), out_tile(0, na), out_tile(na, na), out_tile(2 * na, na),
                   out_tile(3 * na, nu)],
        out_shape=[jax.ShapeDtypeStruct((m, d), BF16), jax.ShapeDtypeStruct((m, aw), BF16),
                   jax.ShapeDtypeStruct((m, aw), F32), jax.ShapeDtypeStruct((m, aw), F32),
                   jax.ShapeDtypeStruct((m, cd), F32)],
        compiler_params=_params(("parallel", "arbitrary"), blocks),
        name="in_proj",
    )(x, g, w_in, w_in)


def _merge(h, o, c, w_in, off, w_attn_out, w_conv_out):
    m, d = h.shape
    aw, cd = o.shape[1], c.shape[1]
    tm, tn = _tile(m, ROW_TILE), _tile(d, COL_TILE)
    assert off % tn == 0
    blocks = [((tm, d), BF16), ((tm, aw), BF16), ((tm, cd), BF16), ((d, tn), BF16), ((d, tn), BF16),
              ((aw, tn), BF16), ((cd, tn), BF16), ((tm, tn), BF16)]
    return pl.pallas_call(
        _merge_kernel,
        grid=(m // tm, d // tn),
        in_specs=[_row_spec(tm, d), _row_spec(tm, aw), _row_spec(tm, cd),
                  _col_spec(d, tn, off // tn), _col_spec(d, tn, (off + d) // tn),
                  _col_spec(aw, tn), _col_spec(cd, tn)],
        out_specs=_tile_spec(tm, tn),
        out_shape=jax.ShapeDtypeStruct((m, d), BF16),
        compiler_params=_params(("parallel", "arbitrary"), blocks),
        name="gated_merge",
    )(h, o, c, w_in, w_in, w_attn_out, w_conv_out)


def _out_proj(mix_in, w_out, x, g_post, g_pre):
    m, d = x.shape
    tm = _tile(m, ROW_TILE // 2)
    row = pl.BlockSpec((tm, d), lambda i: (i, 0))
    vec = pl.BlockSpec((1, d), lambda i: (0, 0))
    blocks = [((tm, d), BF16), ((d, d), BF16), ((tm, d), F32), ((tm, d), F32), ((tm, d), BF16)]
    return pl.pallas_call(
        _out_kernel,
        grid=(m // tm,),
        in_specs=[row, pl.BlockSpec((d, d), lambda i: (0, 0)), row, vec, vec],
        out_specs=[row, row],
        out_shape=[jax.ShapeDtypeStruct((m, d), F32), jax.ShapeDtypeStruct((m, d), BF16)],
        compiler_params=_params(("parallel",), blocks),
        name="out_proj",
    )(mix_in, w_out, x, g_post, g_pre)


def _ffn_up(f, w_gate, w_up):
    m, d = f.shape
    dff = w_gate.shape[1]
    tm, tn = _tile(m, ROW_TILE), _tile(dff, COL_TILE)
    blocks = [((tm, d), BF16), ((d, tn), BF16), ((d, tn), BF16), ((tm, tn), BF16)]
    return pl.pallas_call(
        _ffn_up_kernel,
        grid=(m // tm, dff // tn),
        in_specs=[_row_spec(tm, d), _col_spec(d, tn), _col_spec(d, tn)],
        out_specs=_tile_spec(tm, tn),
        out_shape=jax.ShapeDtypeStruct((m, dff), BF16),
        compiler_params=_params(("parallel", "arbitrary"), blocks),
        name="ffn_up",
    )(f, w_gate, w_up)


def _ffn_down(hd, w_down, x1, g_post):
    m, dff = hd.shape
    d = x1.shape[1]
    tm, tn = _tile(m, ROW_TILE // 2), _tile(d, COL_TILE)
    row = pl.BlockSpec((tm, d), lambda i, j: (i, 0))
    blocks = [((tm, dff), BF16), ((dff, tn), BF16), ((tm, d), F32), ((tm, d), F32)]
    scratch = [((tm, d), F32)]
    return pl.pallas_call(
        _ffn_down_kernel,
        grid=(m // tm, d // tn),
        in_specs=[_row_spec(tm, dff), _col_spec(dff, tn), row, _vec_spec(d)],
        out_specs=row,
        out_shape=jax.ShapeDtypeStruct((m, d), F32),
        scratch_shapes=[pltpu.VMEM(s, dt) for s, dt in scratch],
        compiler_params=_params(("parallel", "arbitrary"), blocks, scratch),
        name="ffn_down",
    )(hd, w_down, x1, g_post)


def _ple(x2, pe, w_gate, w_proj):
    m, d = x2.shape
    pd = pe.shape[1]
    tm, tn = _tile(m, ROW_TILE), _tile(d, COL_TILE)
    blocks = [((tm, d), F32), ((tm, pd), F32), ((d, tn), BF16), ((pd, tn), BF16), ((tm, tn), F32)]
    scratch = [((tm, d), BF16)]
    return pl.pallas_call(
        _ple_kernel,
        grid=(m // tm, d // tn),
        in_specs=[_row_spec(tm, d), _row_spec(tm, pd), _col_spec(d, tn), _col_spec(pd, tn)],
        out_specs=_tile_spec(tm, tn),
        out_shape=jax.ShapeDtypeStruct((m, d), F32),
        scratch_shapes=[pltpu.VMEM(s, dt) for s, dt in scratch],
        compiler_params=_params(("parallel", "arbitrary"), blocks, scratch),
        name="ple",
    )(x2, pe, w_gate, w_proj)


def _lambda_specs(dh, index_map):
    return [pl.BlockSpec((1, dh), index_map)] * 4


def _alibi_key_lanes(t, e, lane0):
    pos = jnp.arange(t, dtype=jnp.int32)[:, None]
    lo = pos % BF16_EXACT_INT
    g = (jnp.arange(e, dtype=jnp.int32)[None, :] - lane0) // BF16_PIECES
    vals = jnp.where(g == 0, pos - lo, jnp.where(g == 1, lo, 1))
    return jnp.where((g >= 0) & (g < 3), vals, 0).astype(BF16)


def _attention(q, k, v, qd, kd, vd, cache_k, cache_v, page_table, slopes, lams, g_subln, lam_init, n_heads):
    b, t, aw = q.shape
    bs = qd.shape[0]
    e = aw // n_heads
    dh = e // 2
    assert e == LANES and dh >= 3 * BF16_PIECES, "ALiBi lanes ride in the unused half of each map's copy"
    assert bs == b * n_heads, "one sample sequence rides on each (batch, head) of the prompt grid"
    tq = _tile(t, ATTN_TILE)
    nq = t // tq
    n_pool, ps = cache_k.shape[:2]
    n_pages = page_table.shape[1]
    assert n_pages % nq == 0, "each query-block step takes an equal share of the pages"
    group = n_pages // nq
    rows, cols = 2 * n_heads, ps * n_heads
    key = jnp.arange(tq, dtype=jnp.int32)[:, None]
    qry = jnp.arange(2 * tq, dtype=jnp.int32)[None, :] % tq
    mask = jnp.where(key <= qry, 0.0, NEG_INF).astype(F32)
    col = jnp.arange(group * cols, dtype=jnp.int32)[None, :]

    const = lambda bi, h, qi, pt: (0, 0)
    seq = pl.BlockSpec((1, t, e), lambda bi, h, qi, pt: (bi, 0, h))
    tok = pl.BlockSpec((1, n_heads, e), lambda bi, h, qi, pt: (bi * n_heads + h, 0, 0))
    pages = [pl.BlockSpec((1, cols, e),
                          lambda bi, h, qi, pt, g=g: (pt[(bi * n_heads + h) * n_pages + qi * group + g], 0, 0))
             for g in range(group)]
    blocks = [((t, e), F32), ((t, e), F32), ((t, e), BF16), ((t, e), BF16), ((t, e), BF16), ((t, e), BF16),
              ((tq, 2 * tq), F32)] + [((cols, e), F32)] * (2 * group)
    scratch = [((t, e), BF16), ((t, e), BF16), ((e, t), BF16), ((tq, 2 * tq), F32), ((tq, 2 * tq), F32),
               ((tq, 2 * tq), BF16), ((e, 2 * tq), F32),
               ((rows, e), BF16), ((rows, group * cols), F32), ((rows, 1), F32), ((rows, 1), F32), ((rows, e), F32)]
    grid_spec = pltpu.PrefetchScalarGridSpec(
        num_scalar_prefetch=1,
        grid=(b, n_heads, nq),
        in_specs=[pl.BlockSpec(memory_space=pltpu.SMEM)] + _lambda_specs(dh, const)
        + [pl.BlockSpec((1, e), const), pl.BlockSpec((t, e), const), pl.BlockSpec((t, e), const),
           pl.BlockSpec((tq, 2 * tq), const), seq, seq, seq,
           pl.BlockSpec((rows, 1), const), pl.BlockSpec((1, group * cols), const),
           pl.BlockSpec((1, group * cols), const), tok, tok, tok] + pages + pages,
        out_specs=[seq, tok],
        scratch_shapes=[pltpu.VMEM(s, d) for s, d in scratch],
    )
    pool_k = cache_k.reshape(n_pool, cols, e)
    pool_v = cache_v.reshape(n_pool, cols, e)
    o, od = pl.pallas_call(
        functools.partial(_attn_kernel, lam_init=lam_init, dh=dh, t=tq, group=group),
        grid_spec=grid_spec,
        out_shape=[jax.ShapeDtypeStruct((b, t, aw), BF16), jax.ShapeDtypeStruct((bs, n_heads, e), BF16)],
        compiler_params=_params(("parallel", "parallel", "arbitrary"), blocks, scratch),
        name="attention",
    )(page_table.reshape(-1), slopes, *lams, g_subln, _alibi_key_lanes(t, e, dh), _alibi_key_lanes(t, e, 0), mask,
      q, k, v, (jnp.tile(slopes, 2) * LOG2E)[:, None], col % n_heads, (col // n_heads).astype(F32),
      qd.reshape(bs, n_heads, e), kd.reshape(bs, n_heads, e), vd.reshape(bs, n_heads, e),
      *([pool_k] * group), *([pool_v] * group))
    return o, od.reshape(bs, aw)


def _prompt_conv(u, conv_w, conv_b, g, b):
    bsz, t, ch = u.shape
    width = conv_w.shape[0]
    hist = -(-(width - 1) // SUBLANES) * SUBLANES
    tt = _tile(t, CONV_ROWS)
    assert tt % hist == 0
    const = lambda bi, ti: (0, 0)
    vec = pl.BlockSpec((1, ch), const)
    blocks = [((hist, ch), F32), ((tt, ch), F32), ((width, ch), F32), ((tt, ch), BF16)]
    scratch = [((hist + tt, ch), F32), ((tt, ch), F32), ((SUBLANES, hist + tt, LANES), F32)]
    return pl.pallas_call(
        _prompt_conv_kernel,
        grid=(bsz, t // tt),
        in_specs=[pl.BlockSpec((1, hist, ch), lambda bi, ti: (bi, jnp.maximum(ti * (tt // hist) - 1, 0), 0)),
                  pl.BlockSpec((1, tt, ch), lambda bi, ti: (bi, ti, 0)),
                  pl.BlockSpec((width, ch), const), vec, vec, vec],
        out_specs=pl.BlockSpec((1, tt, ch), lambda bi, ti: (bi, ti, 0)),
        out_shape=jax.ShapeDtypeStruct((bsz, t, ch), BF16),
        scratch_shapes=[pltpu.VMEM(s, d) for s, d in scratch],
        compiler_params=_params(("parallel", "arbitrary"), blocks, scratch),
        name="prompt_conv",
    )(u, u, conv_w, conv_b, g, b)


def _sample_conv(state, u, conv_w, conv_b, g, b):
    hist, bs, ch = state.shape
    width = conv_w.shape[0]
    assert hist == width - 1
    bt = _tile(bs, 4 * SUBLANES) if bs % SUBLANES == 0 else bs
    vec = pl.BlockSpec((1, ch), lambda i: (0, 0))
    st = pl.BlockSpec((hist, bt, ch), lambda i: (0, i, 0))
    blocks = [((hist, bt, ch), F32), ((bt, ch), F32), ((width, ch), F32), ((bt, ch), BF16), ((hist, bt, ch), F32)]
    return pl.pallas_call(
        _sample_conv_kernel,
        grid=(bs // bt,),
        in_specs=[st, pl.BlockSpec((bt, ch), lambda i: (i, 0)), pl.BlockSpec((width, ch), lambda i: (0, 0)),
                  vec, vec, vec],
        out_specs=[pl.BlockSpec((bt, ch), lambda i: (i, 0)), st],
        out_shape=[jax.ShapeDtypeStruct((bs, ch), BF16), jax.ShapeDtypeStruct((hist, bs, ch), F32)],
        compiler_params=_params(("parallel",), blocks),
        name="sample_conv",
    )(state, u, conv_w, conv_b, g, b)


def _layer_tail(x, pe, h, o, c, lw, dims):
    aw, cd = dims["attn_width"], dims["conv_dim"]
    mix_in = _merge(h, o, c, lw["w_in"], 3 * aw + 2 * cd, lw["w_attn_out"], lw["w_conv_out"])
    x1, f = _out_proj(mix_in, lw["w_out"], x, lw["g_post_mix"], lw["g_pre_ffn"])
    hd = _ffn_up(f, lw["w_ffn_gate"], lw["w_ffn_up"])
    x2 = _ffn_down(hd, lw["w_ffn_down"], x1, lw["g_post_ffn"])
    return _ple(x2, pe, lw["w_ple_gate"], lw["w_ple_proj"])


_MATRICES = ("w_in", "w_attn_out", "w_conv_out", "w_out", "w_ffn_gate", "w_ffn_up", "w_ffn_down",
             "w_ple_proj", "w_ple_gate")
_VECTORS = ("g_pre_mix", "g_subln", "conv_b", "g_conv_norm", "b_conv_norm", "g_post_mix", "g_pre_ffn",
            "g_post_ffn", "lambda_q1", "lambda_k1", "lambda_q2", "lambda_k2")


def kernel(x_prompt, x_sample, cache_k, cache_v, state_conv, page_table, p_prompt, p_sample, g_pre_mix, w_in, lambda_q1, lambda_k1, lambda_q2, lambda_k2, g_subln, w_attn_out, conv_w, conv_b, g_conv_norm, b_conv_norm, w_conv_out, w_out, g_post_mix, g_pre_ffn, w_ffn_gate, w_ffn_up, w_ffn_down, g_post_ffn, w_ple_proj, w_ple_gate):
    weights = dict(g_pre_mix=g_pre_mix, w_in=w_in, lambda_q1=lambda_q1, lambda_k1=lambda_k1,
                   lambda_q2=lambda_q2, lambda_k2=lambda_k2, g_subln=g_subln, w_attn_out=w_attn_out,
                   conv_w=conv_w, conv_b=conv_b, g_conv_norm=g_conv_norm, b_conv_norm=b_conv_norm,
                   w_conv_out=w_conv_out, w_out=w_out, g_post_mix=g_post_mix, g_pre_ffn=g_pre_ffn,
                   w_ffn_gate=w_ffn_gate, w_ffn_up=w_ffn_up, w_ffn_down=w_ffn_down, g_post_ffn=g_post_ffn,
                   w_ple_proj=w_ple_proj, w_ple_gate=w_ple_gate)
    depth = w_in.shape[0]
    bp, t, d = x_prompt.shape
    bs, ts, _ = x_sample.shape
    assert ts == 1, "the sample group decodes one token per sequence"
    n_heads, e = cache_k.shape[3], cache_k.shape[4]
    dims = dict(attn_width=n_heads * e, conv_dim=conv_w.shape[2], attn_scale=(e // 2) ** -0.5 * LOG2E)
    slopes = 2.0 ** (-8.0 * jnp.arange(1, n_heads + 1, dtype=F32) / n_heads)

    yp = x_prompt.reshape(bp * t, d)
    ys = x_sample.reshape(bs, d)
    n_pool = cache_k.shape[1]
    pool_k = cache_k.reshape((depth * n_pool,) + cache_k.shape[2:])
    pool_v = cache_v.reshape((depth * n_pool,) + cache_v.shape[2:])
    outs = [[] for _ in range(6)]
    for i in range(depth):
        lam_init = 0.8 - 0.6 * math.exp(-0.3 * i)
        lw = {n: weights[n][i].astype(BF16) for n in _MATRICES}
        lw.update({n: weights[n][i][None, :] for n in _VECTORS})
        lw["conv_w"] = conv_w[i]
        lams = [lw[n] for n in ("lambda_q1", "lambda_k1", "lambda_q2", "lambda_k2")]
        ln = (lw["conv_b"], lw["g_conv_norm"], lw["b_conv_norm"])

        aw, cd = dims["attn_width"], dims["conv_dim"]
        hp, qp, kp, vp, up = _in_proj(yp, lw["g_pre_mix"], lw["w_in"], aw, cd, dims["attn_scale"])
        hs, qs, ksn, vsn, us = _in_proj(ys, lw["g_pre_mix"], lw["w_in"], aw, cd, dims["attn_scale"])
        shape = (bp, t, aw)
        op, os_ = _attention(qp.reshape(shape), kp.reshape(shape), vp.reshape(shape), qs, ksn, vsn, pool_k, pool_v,
                             page_table + i * n_pool, slopes, lams, lw["g_subln"], lam_init, n_heads)
        up3 = up.reshape(bp, t, cd)
        cp_act = _prompt_conv(up3, lw["conv_w"], *ln).reshape(bp * t, cd)
        cp = up3[:, t - (conv_w.shape[1] - 1):, :]
        cs_act, csn = _sample_conv(jnp.swapaxes(state_conv[i], 0, 1), us, lw["conv_w"], *ln)
        csn = jnp.swapaxes(csn, 0, 1)
        yp = _layer_tail(yp, p_prompt[i].reshape(bp * t, -1), hp, op.reshape(bp * t, aw), cp_act, lw, dims)
        ys = _layer_tail(ys, p_sample[i].reshape(bs, -1), hs, os_, cs_act, lw, dims)
        for lst, val in zip(outs, (kp.reshape(bp, t, n_heads, e), vp.reshape(bp, t, n_heads, e), cp,
                                   ksn.reshape(bs, 1, n_heads, e), vsn.reshape(bs, 1, n_heads, e), csn)):
            lst.append(val)
    stack = (lambda o: o[0][None]) if depth == 1 else jnp.stack
    return (yp.reshape(bp, t, d), ys.reshape(bs, 1, d), *[stack(o) for o in outs])
```

```python
import functools
import math

import jax
import jax.numpy as jnp
from jax import lax
from jax.experimental import pallas as pl
from jax.experimental.pallas import tpu as pltpu

F32 = jnp.float32
BF16 = jnp.bfloat16

EPS = 1e-6
NEG_INF = -1e30

LANES = 128
SUBLANES = 8
VMEM_BYTES_V7X = 64 * 1024 * 1024
VMEM_CEILING = VMEM_BYTES_V7X - 8 * 1024 * 1024
VMEM_INTERNAL = 12 * 1024 * 1024

ROW_TILE = 1024
COL_TILE = 512
NORM_ROWS = 256
ATTN_TILE = 512
CONV_ROWS = 128

LOG2E = 1.4426950408889634
BF16_EXACT_INT = 256
BF16_PIECES = 3


def _tile(n, pref):
    if n <= pref:
        return n
    if pref < LANES:
        assert n % pref == 0, (n, pref)
        return pref
    t = (pref // LANES) * LANES
    while t > LANES and n % t:
        t -= LANES
    assert n % t == 0, (n, pref)
    return t


def _nbytes(shape, dtype):
    return math.prod(shape) * jnp.dtype(dtype).itemsize


def _params(semantics, blocks, scratch=()):
    need = 2 * sum(_nbytes(s, d) for s, d in blocks)
    need += sum(_nbytes(s, d) for s, d in scratch) + VMEM_INTERNAL
    return pltpu.CompilerParams(dimension_semantics=semantics,
                                vmem_limit_bytes=min(need, VMEM_CEILING))


def _dot(a, b):
    return jnp.dot(a, b, preferred_element_type=F32)


def _dot_nt(a, b):
    return lax.dot_general(a, b, (((1,), (1,)), ((), ())), preferred_element_type=F32)


def _rms(x, g):
    return x * lax.rsqrt(jnp.mean(x * x, axis=-1, keepdims=True) + EPS) * g


def _lambda(lq1, lk1, lq2, lk2, lam_init):
    s1 = jnp.sum(lq1[...] * lk1[...], axis=-1, keepdims=True)
    s2 = jnp.sum(lq2[...] * lk2[...], axis=-1, keepdims=True)
    return jnp.exp(s1) - jnp.exp(s2) + lam_init


def _in_proj_kernel(x_ref, g_ref, wa_ref, wb_ref, h_ref, hm_ref, k_ref, v_ref, u_ref, *, scale, n_attn):
    j = pl.program_id(1)
    e = hm_ref.shape[2]

    def head_major(y):
        for hh in range(hm_ref.shape[0]):
            hm_ref[hh] = y[:, hh * e:(hh + 1) * e].astype(BF16)

    @pl.when(j == 0)
    def _():
        rows = _tile(x_ref.shape[0], NORM_ROWS)
        for r in range(0, x_ref.shape[0], rows):
            h_ref[pl.ds(r, rows), :] = _rms(x_ref[pl.ds(r, rows), :], g_ref[...]).astype(BF16)

    @pl.when(j < n_attn)
    def _():
        head_major(_dot(h_ref[...], wa_ref[...]) * scale)

    @pl.when((j >= n_attn) & (j < 2 * n_attn))
    def _():
        y = _dot(h_ref[...], wa_ref[...])
        k_ref[...] = y
        head_major(y)

    @pl.when((j >= 2 * n_attn) & (j < 3 * n_attn))
    def _():
        y = _dot(h_ref[...], wa_ref[...])
        v_ref[...] = y
        head_major(y)

    @pl.when(j >= 3 * n_attn)
    def _():
        h = h_ref[...]
        u_ref[...] = _dot(h, wa_ref[...]) * jax.nn.sigmoid(_dot(h, wb_ref[...]))


def _merge_kernel(h_ref, o_ref, c_ref, wga_ref, wgc_ref, wa_ref, wc_ref, m_ref):
    h = h_ref[...]
    a = _dot(o_ref[...], wa_ref[...])
    c = _dot(c_ref[...], wc_ref[...])
    m = jax.nn.sigmoid(_dot(h, wga_ref[...])) * a + jax.nn.sigmoid(_dot(h, wgc_ref[...])) * c
    m_ref[...] = m.astype(BF16)


def _out_kernel(m_ref, w_ref, x_ref, gpost_ref, gpre_ref, x1_ref, f_ref):
    rows = _tile(m_ref.shape[0], NORM_ROWS)
    for r in range(0, m_ref.shape[0], rows):
        sl = pl.ds(r, rows)
        x1 = x_ref[sl, :] + _rms(_dot(m_ref[sl, :], w_ref[...]), gpost_ref[...])
        x1_ref[sl, :] = x1
        f_ref[sl, :] = _rms(x1, gpre_ref[...]).astype(BF16)


def _ffn_up_kernel(f_ref, wg_ref, wu_ref, hd_ref):
    f = f_ref[...]
    hd_ref[...] = (jax.nn.silu(_dot(f, wg_ref[...])) * _dot(f, wu_ref[...])).astype(BF16)


def _ffn_down_kernel(hd_ref, w_ref, x1_ref, g_ref, x2_ref, y_ref):
    j = pl.program_id(1)
    tn = w_ref.shape[1]
    y_ref[:, pl.ds(pl.multiple_of(j * tn, tn), tn)] = _dot(hd_ref[...], w_ref[...])

    @pl.when(j == pl.num_programs(1) - 1)
    def _():
        rows = _tile(y_ref.shape[0], NORM_ROWS)
        for r in range(0, y_ref.shape[0], rows):
            sl = pl.ds(r, rows)
            x2_ref[sl, :] = x1_ref[sl, :] + _rms(y_ref[sl, :], g_ref[...])


def _ple_kernel(x2_ref, pe_ref, wg_ref, wp_ref, y_ref, x2b_ref):
    j = pl.program_id(1)
    tn = y_ref.shape[1]

    @pl.when(j == 0)
    def _():
        x2b_ref[...] = x2_ref[...].astype(BF16)

    gate = jax.nn.sigmoid(_dot(x2b_ref[...], wg_ref[...]))
    res = x2_ref[:, pl.ds(pl.multiple_of(j * tn, tn), tn)]
    y_ref[...] = res + gate * _dot(pe_ref[...].astype(BF16), wp_ref[...])


def _bf16_pieces(x):
    pieces = []
    for _ in range(BF16_PIECES):
        p = x.astype(BF16).astype(F32)
        pieces.append(p)
        x = x - p
    return pieces


def _alibi_lanes(lane0, lane, groups):
    out = jnp.zeros(lane.shape, F32)
    for g, pieces in enumerate(groups):
        for i, p in enumerate(pieces):
            out = jnp.where(lane == lane0 + g * BF16_PIECES + i, p, out)
    return out


def _attn_kernel(pt_ref, slopes_ref, lq1, lk1, lq2, lk2, gsub_ref, kx1_ref, kx2_ref, mask_ref, q_ref, k_ref,
                 v_ref, sig_ref, khead_ref, kpos_ref, qd_ref, kn_ref, vn_ref, *refs, lam_init, dh, t, group):
    del pt_ref
    kc_refs, vc_refs = refs[:group], refs[group:2 * group]
    (o_ref, od_ref, k1_ref, k2_ref, vt_ref, sa_ref, sb_ref, p_ref, acc_ref,
     qs_ref, bias_ref, dm_ref, dl_ref, dacc_ref) = refs[2 * group:]
    seq, e = k_ref.shape[2], k_ref.shape[3]
    n_blocks = seq // t
    qi = pl.program_id(2)
    sigma = slopes_ref[pl.program_id(1)] * LOG2E
    n_heads = qd_ref.shape[1]
    cols = kc_refs[0].shape[1]
    ps = cols // n_heads
    dsig = sig_ref[...]

    @pl.when(qi == 0)
    def _():
        k = k_ref[0, 0]
        lane = lax.broadcasted_iota(jnp.int32, k.shape, 1)
        k1_ref[...] = jnp.where(lane < dh, k, kx1_ref[...])
        k2_ref[...] = jnp.where(lane >= dh, k, kx2_ref[...])
        vt_ref[...] = v_ref[0, 0].astype(F32).T.astype(BF16)
        qd = qd_ref[0]
        lane = lax.broadcasted_iota(jnp.int32, qd.shape, 1)
        zero = jnp.zeros_like(qd)
        qs_ref[...] = jnp.concatenate([jnp.where(lane < dh, qd, zero), jnp.where(lane >= dh, qd, zero)], axis=0)
        row_head = lax.broadcasted_iota(jnp.int32, bias_ref.shape, 0) % n_heads
        bias_ref[...] = jnp.where(khead_ref[...] == row_head, dsig * kpos_ref[...], NEG_INF)
        dm_ref[...] = jnp.full_like(dm_ref, NEG_INF)
        dl_ref[...] = jnp.zeros_like(dl_ref)
        dacc_ref[...] = jnp.zeros_like(dacc_ref)

    lam = _lambda(lq1, lk1, lq2, lk2, lam_init)

    def queries(c):
        q = q_ref[0, 0, pl.ds(c * t, t), :].astype(F32)
        lane = lax.broadcasted_iota(jnp.int32, (t, e), 1)
        qpos = (c * t + lax.broadcasted_iota(jnp.int32, (t, e), 0)).astype(F32)
        sig = jnp.full((t, e), sigma, F32)
        sig_p = _bf16_pieces(sig)
        groups = [sig_p, sig_p, _bf16_pieces(-sig * qpos)]
        return (jnp.where(lane < dh, q, _alibi_lanes(dh, lane, groups)).T.astype(BF16),
                jnp.where(lane >= dh, q, _alibi_lanes(0, lane, groups)).T.astype(BF16))

    def scores(s_ref, qs, kj):
        keys = pl.ds(kj * t, t)
        s_ref[:, pl.ds(0, t)] = _dot(k1_ref[keys, :], qs[0])
        s_ref[:, pl.ds(t, t)] = _dot(k2_ref[keys, :], qs[1])

    def update(s_ref, kj, m, l, first, masked):
        m_out, l_out, alphas = [], [], []
        for c in range(2 * t // LANES):
            lanes = pl.ds(c * LANES, LANES)
            sc = s_ref[:, lanes]
            if masked:
                sc = sc + mask_ref[:, lanes]
            mc = jnp.max(sc, axis=0, keepdims=True)
            if not first:
                mc = jnp.maximum(m[c], mc)
            p = jnp.exp2(sc - mc)
            lc = jnp.sum(p, axis=0, keepdims=True)
            if not first:
                alpha = jnp.exp2(m[c] - mc)
                lc = alpha * l[c] + lc
                alphas.append(alpha)
            p_ref[:, lanes] = p.astype(BF16)
            m_out.append(mc)
            l_out.append(lc)
        pv = _dot(vt_ref[:, pl.ds(kj * t, t)], p_ref[...])
        if first:
            acc_ref[...] = pv
        else:
            acc_ref[...] = jnp.concatenate(alphas, axis=1) * acc_ref[...] + pv
        return m_out, l_out

    def prompt_block(c):
        score_refs = (sa_ref, sb_ref)
        qs = queries(c)
        scores(score_refs[0], qs, 0)
        m = l = None
        for kj in range(c + 1):
            if kj < c:
                scores(score_refs[(kj + 1) % 2], qs, kj + 1)
            m, l = update(score_refs[kj % 2], kj, m, l, first=(kj == 0), masked=(kj == c))
        l_row = jnp.concatenate(l, axis=1)
        acc = acc_ref[...]
        o = (acc[:, :t] / l_row[:, :t] - lam * (acc[:, t:] / l_row[:, t:])).T
        o_ref[0, pl.ds(c * t, t), :] = (_rms(o, gsub_ref[...]) * (1.0 - lam_init)).astype(BF16)

    def sample_pages(c):
        shift = -dsig * float((n_blocks - c) * group * ps)
        qs = qs_ref[...]
        s = jnp.concatenate([_dot_nt(qs, kc[0].astype(BF16)) for kc in kc_refs], axis=1) + bias_ref[...]
        m_old = dm_ref[...]
        m_new = jnp.maximum(m_old, jnp.max(s, axis=-1, keepdims=True) + shift)
        p = jnp.exp2(s - (m_new - shift))
        alpha = jnp.exp2(m_old - m_new)
        dm_ref[...] = m_new
        dl_ref[...] = alpha * dl_ref[...] + jnp.sum(p, axis=-1, keepdims=True)
        p = p.astype(BF16)
        pv = _dot(p[:, :cols], vc_refs[0][0].astype(BF16))
        for g in range(1, group):
            pv = pv + _dot(p[:, g * cols:(g + 1) * cols], vc_refs[g][0].astype(BF16))
        dacc_ref[...] = alpha * dacc_ref[...] + pv

    for c in range(n_blocks):
        @pl.when(qi == c)
        def _(c=c):
            sample_pages(c)
            prompt_block(c)

    @pl.when(qi == n_blocks - 1)
    def _():
        prod = qd_ref[0].astype(F32) * kn_ref[0].astype(BF16).astype(F32)
        lane = lax.broadcasted_iota(jnp.int32, prod.shape, 1)
        s1 = jnp.sum(jnp.where(lane < dh, prod, 0.0), axis=-1, keepdims=True)
        s2 = jnp.sum(jnp.where(lane >= dh, prod, 0.0), axis=-1, keepdims=True)
        s_self = jnp.concatenate([s1, s2], axis=0)
        vn = vn_ref[0].astype(BF16).astype(F32)
        m_old = dm_ref[...]
        m_fin = jnp.maximum(m_old, s_self)
        alpha = jnp.exp2(m_old - m_fin)
        p_self = jnp.exp2(s_self - m_fin)
        l = alpha * dl_ref[...] + p_self
        acc = alpha * dacc_ref[...] + p_self * jnp.concatenate([vn, vn], axis=0)
        a = acc / l
        o = a[:n_heads] - lam * a[n_heads:]
        od_ref[0] = (_rms(o, gsub_ref[...]) * (1.0 - lam_init)).astype(BF16)


def _ln_silu(c, g, b):
    mu = jnp.mean(c, axis=-1, keepdims=True)
    var = jnp.mean(jnp.square(c - mu), axis=-1, keepdims=True)
    y = (c - mu) * lax.rsqrt(var + EPS) * g + b
    return jax.nn.silu(y)


def _prompt_conv_kernel(prev_ref, cur_ref, w_ref, cb_ref, g_ref, b_ref, o_ref, full_ref, c_ref, sh_ref):
    hist = prev_ref.shape[1]
    tt, ch = c_ref.shape
    width = w_ref.shape[0]

    @pl.when(pl.program_id(1) == 0)
    def _():
        full_ref[pl.ds(0, hist), :] = jnp.zeros((hist, ch), F32)

    @pl.when(pl.program_id(1) > 0)
    def _():
        full_ref[pl.ds(0, hist), :] = prev_ref[0]

    full_ref[pl.ds(hist, tt), :] = cur_ref[0]

    base = hist - (width - 1)

    def chunk(cc, _):
        lanes = pl.ds(pl.multiple_of(cc * LANES, LANES), LANES)
        acc = jnp.broadcast_to(cb_ref[:, lanes], (tt, LANES))
        for r in range(min(SUBLANES, width)):
            taps = range(r, width, SUBLANES)
            span = tt + taps[-1] - r
            sh_ref[r, pl.ds(0, span), :] = full_ref[pl.ds(base + r, span), lanes]
            for k in taps:
                acc = acc + w_ref[pl.ds(k, 1), lanes] * sh_ref[r, pl.ds(k - r, tt), :]
        c_ref[:, lanes] = acc
        return 0

    lax.fori_loop(0, ch // LANES, chunk, 0)
    o_ref[0] = _ln_silu(c_ref[...], g_ref[...], b_ref[...]).astype(BF16)


def _sample_conv_kernel(state_ref, u_ref, w_ref, cb_ref, g_ref, b_ref, o_ref, ns_ref):
    hist = state_ref.shape[0]
    u = u_ref[...]
    c = u * w_ref[pl.ds(hist, 1), :] + cb_ref[...]
    for k in range(hist):
        c = c + state_ref[k] * w_ref[pl.ds(k, 1), :]
    o_ref[...] = _ln_silu(c, g_ref[...], b_ref[...]).astype(BF16)
    for k in range(hist - 1):
        ns_ref[k] = state_ref[k + 1]
    ns_ref[hist - 1] = u


def _row_spec(tm, n):
    return pl.BlockSpec((tm, n), lambda i, j: (i, 0))


def _col_spec(k, tn, off_blocks=0):
    return pl.BlockSpec((k, tn), lambda i, j: (0, j + off_blocks))


def _tile_spec(tm, tn):
    return pl.BlockSpec((tm, tn), lambda i, j: (i, j))


def _vec_spec(n):
    return pl.BlockSpec((1, n), lambda i, j: (0, 0))


def _in_proj(x, g, w_in, aw, cd, e, scale):
    m, d = x.shape
    tm, tn = _tile(m, ROW_TILE), _tile(math.gcd(aw, cd), COL_TILE)
    assert tn % e == 0
    na, nu, hp = aw // tn, cd // tn, tn // e
    gate0 = (3 * aw + cd) // tn

    def out_tile(first, count):
        return pl.BlockSpec((tm, tn), lambda i, j: (i, jnp.clip(j - first, 0, count - 1)))

    blocks = [((tm, d), F32), ((1, d), F32), ((d, tn), BF16), ((d, tn), BF16), ((tm, d), BF16),
              ((tm, tn), BF16), ((tm, tn), F32), ((tm, tn), F32), ((tm, tn), F32)]
    return pl.pallas_call(
        functools.partial(_in_proj_kernel, scale=scale, n_attn=na),
        grid=(m // tm, 3 * na + nu),
        in_specs=[_row_spec(tm, d), _vec_spec(d), _col_spec(d, tn),
                  pl.BlockSpec((d, tn), lambda i, j: (0, gate0 + jnp.maximum(j - 3 * na, 0)))],
        out_specs=[_row_spec(tm, d),
                   pl.BlockSpec((hp, tm, e), lambda i, j: (jnp.minimum(j, 3 * na - 1), i, 0)),
                   out_tile(na, na), out_tile(2 * na, na), out_tile(3 * na, nu)],
        out_shape=[jax.ShapeDtypeStruct((m, d), BF16), jax.ShapeDtypeStruct((3 * aw // e, m, e), BF16),
                   jax.ShapeDtypeStruct((m, aw), F32), jax.ShapeDtypeStruct((m, aw), F32),
                   jax.ShapeDtypeStruct((m, cd), F32)],
        compiler_params=_params(("parallel", "arbitrary"), blocks),
        name="in_proj",
    )(x, g, w_in, w_in)


def _merge(h, o, c, w_in, off, w_attn_out, w_conv_out):
    m, d = h.shape
    aw, cd = o.shape[1], c.shape[1]
    tm, tn = _tile(m, ROW_TILE), _tile(d, COL_TILE)
    assert off % tn == 0
    blocks = [((tm, d), BF16), ((tm, aw), BF16), ((tm, cd), BF16), ((d, tn), BF16), ((d, tn), BF16),
              ((aw, tn), BF16), ((cd, tn), BF16), ((tm, tn), BF16)]
    return pl.pallas_call(
        _merge_kernel,
        grid=(m // tm, d // tn),
        in_specs=[_row_spec(tm, d), _row_spec(tm, aw), _row_spec(tm, cd),
                  _col_spec(d, tn, off // tn), _col_spec(d, tn, (off + d) // tn),
                  _col_spec(aw, tn), _col_spec(cd, tn)],
        out_specs=_tile_spec(tm, tn),
        out_shape=jax.ShapeDtypeStruct((m, d), BF16),
        compiler_params=_params(("parallel", "arbitrary"), blocks),
        name="gated_merge",
    )(h, o, c, w_in, w_in, w_attn_out, w_conv_out)


def _out_proj(mix_in, w_out, x, g_post, g_pre):
    m, d = x.shape
    tm = _tile(m, ROW_TILE // 2)
    row = pl.BlockSpec((tm, d), lambda i: (i, 0))
    vec = pl.BlockSpec((1, d), lambda i: (0, 0))
    blocks = [((tm, d), BF16), ((d, d), BF16), ((tm, d), F32), ((tm, d), F32), ((tm, d), BF16)]
    return pl.pallas_call(
        _out_kernel,
        grid=(m // tm,),
        in_specs=[row, pl.BlockSpec((d, d), lambda i: (0, 0)), row, vec, vec],
        out_specs=[row, row],
        out_shape=[jax.ShapeDtypeStruct((m, d), F32), jax.ShapeDtypeStruct((m, d), BF16)],
        compiler_params=_params(("parallel",), blocks),
        name="out_proj",
    )(mix_in, w_out, x, g_post, g_pre)


def _ffn_up(f, w_gate, w_up):
    m, d = f.shape
    dff = w_gate.shape[1]
    tm, tn = _tile(m, ROW_TILE), _tile(dff, COL_TILE)
    blocks = [((tm, d), BF16), ((d, tn), BF16), ((d, tn), BF16), ((tm, tn), BF16)]
    return pl.pallas_call(
        _ffn_up_kernel,
        grid=(m // tm, dff // tn),
        in_specs=[_row_spec(tm, d), _col_spec(d, tn), _col_spec(d, tn)],
        out_specs=_tile_spec(tm, tn),
        out_shape=jax.ShapeDtypeStruct((m, dff), BF16),
        compiler_params=_params(("parallel", "arbitrary"), blocks),
        name="ffn_up",
    )(f, w_gate, w_up)


def _ffn_down(hd, w_down, x1, g_post):
    m, dff = hd.shape
    d = x1.shape[1]
    tm, tn = _tile(m, ROW_TILE // 2), _tile(d, COL_TILE)
    row = pl.BlockSpec((tm, d), lambda i, j: (i, 0))
    blocks = [((tm, dff), BF16), ((dff, tn), BF16), ((tm, d), F32), ((tm, d), F32)]
    scratch = [((tm, d), F32)]
    return pl.pallas_call(
        _ffn_down_kernel,
        grid=(m // tm, d // tn),
        in_specs=[_row_spec(tm, dff), _col_spec(dff, tn), row, _vec_spec(d)],
        out_specs=row,
        out_shape=jax.ShapeDtypeStruct((m, d), F32),
        scratch_shapes=[pltpu.VMEM(s, dt) for s, dt in scratch],
        compiler_params=_params(("parallel", "arbitrary"), blocks, scratch),
        name="ffn_down",
    )(hd, w_down, x1, g_post)


def _ple(x2, pe, w_gate, w_proj):
    m, d = x2.shape
    pd = pe.shape[1]
    tm, tn = _tile(m, ROW_TILE), _tile(d, COL_TILE)
    blocks = [((tm, d), F32), ((tm, pd), F32), ((d, tn), BF16), ((pd, tn), BF16), ((tm, tn), F32)]
    scratch = [((tm, d), BF16)]
    return pl.pallas_call(
        _ple_kernel,
        grid=(m // tm, d // tn),
        in_specs=[_row_spec(tm, d), _row_spec(tm, pd), _col_spec(d, tn), _col_spec(pd, tn)],
        out_specs=_tile_spec(tm, tn),
        out_shape=jax.ShapeDtypeStruct((m, d), F32),
        scratch_shapes=[pltpu.VMEM(s, dt) for s, dt in scratch],
        compiler_params=_params(("parallel", "arbitrary"), blocks, scratch),
        name="ple",
    )(x2, pe, w_gate, w_proj)


def _lambda_specs(dh, index_map):
    return [pl.BlockSpec((1, dh), index_map)] * 4


def _alibi_key_lanes(t, e, lane0):
    pos = jnp.arange(t, dtype=jnp.int32)[:, None]
    lo = pos % BF16_EXACT_INT
    g = (jnp.arange(e, dtype=jnp.int32)[None, :] - lane0) // BF16_PIECES
    vals = jnp.where(g == 0, pos - lo, jnp.where(g == 1, lo, 1))
    return jnp.where((g >= 0) & (g < 3), vals, 0).astype(BF16)


def _attention(qkv, qd, kd, vd, cache_k, cache_v, page_table, slopes, lams, g_subln, lam_init, n_heads):
    _, b, t, e = qkv.shape
    bs = qd.shape[0]
    aw = n_heads * e
    dh = e // 2
    assert e == LANES and dh >= 3 * BF16_PIECES, "ALiBi lanes ride in the unused half of each map's copy"
    assert bs == b * n_heads, "one sample sequence rides on each (batch, head) of the prompt grid"
    tq = _tile(t, ATTN_TILE)
    nq = t // tq
    n_pool, ps = cache_k.shape[:2]
    n_pages = page_table.shape[1]
    assert n_pages % nq == 0, "each query-block step takes an equal share of the pages"
    group = n_pages // nq
    rows, cols = 2 * n_heads, ps * n_heads
    key = jnp.arange(tq, dtype=jnp.int32)[:, None]
    qry = jnp.arange(2 * tq, dtype=jnp.int32)[None, :] % tq
    mask = jnp.where(key <= qry, 0.0, NEG_INF).astype(F32)
    col = jnp.arange(group * cols, dtype=jnp.int32)[None, :]

    const = lambda bi, h, qi, pt: (0, 0)
    seq = pl.BlockSpec((1, t, e), lambda bi, h, qi, pt: (bi, 0, h))
    slab = [pl.BlockSpec((1, 1, t, e), lambda bi, h, qi, pt, part=part: (part * n_heads + h, bi, 0, 0))
            for part in range(3)]
    tok = pl.BlockSpec((1, n_heads, e), lambda bi, h, qi, pt: (bi * n_heads + h, 0, 0))
    pages = [pl.BlockSpec((1, cols, e),
                          lambda bi, h, qi, pt, g=g: (pt[(bi * n_heads + h) * n_pages + qi * group + g], 0, 0))
             for g in range(group)]
    blocks = [((t, e), BF16)] * 6 + [((tq, 2 * tq), F32)] + [((cols, e), F32)] * (2 * group)
    scratch = [((t, e), BF16), ((t, e), BF16), ((e, t), BF16), ((tq, 2 * tq), F32), ((tq, 2 * tq), F32),
               ((tq, 2 * tq), BF16), ((e, 2 * tq), F32),
               ((rows, e), BF16), ((rows, group * cols), F32), ((rows, 1), F32), ((rows, 1), F32), ((rows, e), F32)]
    grid_spec = pltpu.PrefetchScalarGridSpec(
        num_scalar_prefetch=1,
        grid=(b, n_heads, nq),
        in_specs=[pl.BlockSpec(memory_space=pltpu.SMEM)] + _lambda_specs(dh, const)
        + [pl.BlockSpec((1, e), const), pl.BlockSpec((t, e), const), pl.BlockSpec((t, e), const),
           pl.BlockSpec((tq, 2 * tq), const)] + slab
        + [pl.BlockSpec((rows, 1), const), pl.BlockSpec((1, group * cols), const),
           pl.BlockSpec((1, group * cols), const), tok, tok, tok] + pages + pages,
        out_specs=[seq, tok],
        scratch_shapes=[pltpu.VMEM(s, d) for s, d in scratch],
    )
    pool_k = cache_k.reshape(n_pool, cols, e)
    pool_v = cache_v.reshape(n_pool, cols, e)
    o, od = pl.pallas_call(
        functools.partial(_attn_kernel, lam_init=lam_init, dh=dh, t=tq, group=group),
        grid_spec=grid_spec,
        out_shape=[jax.ShapeDtypeStruct((b, t, aw), BF16), jax.ShapeDtypeStruct((bs, n_heads, e), BF16)],
        compiler_params=_params(("parallel", "parallel", "arbitrary"), blocks, scratch),
        name="attention",
    )(page_table.reshape(-1), slopes, *lams, g_subln, _alibi_key_lanes(t, e, dh), _alibi_key_lanes(t, e, 0), mask,
      qkv, qkv, qkv, (jnp.tile(slopes, 2) * LOG2E)[:, None], col % n_heads, (col // n_heads).astype(F32),
      qd, kd.reshape(bs, n_heads, e), vd.reshape(bs, n_heads, e),
      *([pool_k] * group), *([pool_v] * group))
    return o, od.reshape(bs, aw)


def _prompt_conv(u, conv_w, conv_b, g, b):
    bsz, t, ch = u.shape
    width = conv_w.shape[0]
    hist = -(-(width - 1) // SUBLANES) * SUBLANES
    tt = _tile(t, CONV_ROWS)
    assert tt % hist == 0
    const = lambda bi, ti: (0, 0)
    vec = pl.BlockSpec((1, ch), const)
    blocks = [((hist, ch), F32), ((tt, ch), F32), ((width, ch), F32), ((tt, ch), BF16)]
    scratch = [((hist + tt, ch), F32), ((tt, ch), F32), ((SUBLANES, hist + tt, LANES), F32)]
    return pl.pallas_call(
        _prompt_conv_kernel,
        grid=(bsz, t // tt),
        in_specs=[pl.BlockSpec((1, hist, ch), lambda bi, ti: (bi, jnp.maximum(ti * (tt // hist) - 1, 0), 0)),
                  pl.BlockSpec((1, tt, ch), lambda bi, ti: (bi, ti, 0)),
                  pl.BlockSpec((width, ch), const), vec, vec, vec],
        out_specs=pl.BlockSpec((1, tt, ch), lambda bi, ti: (bi, ti, 0)),
        out_shape=jax.ShapeDtypeStruct((bsz, t, ch), BF16),
        scratch_shapes=[pltpu.VMEM(s, d) for s, d in scratch],
        compiler_params=_params(("parallel", "arbitrary"), blocks, scratch),
        name="prompt_conv",
    )(u, u, conv_w, conv_b, g, b)


def _sample_conv(state, u, conv_w, conv_b, g, b):
    hist, bs, ch = state.shape
    width = conv_w.shape[0]
    assert hist == width - 1
    bt = _tile(bs, 4 * SUBLANES) if bs % SUBLANES == 0 else bs
    vec = pl.BlockSpec((1, ch), lambda i: (0, 0))
    st = pl.BlockSpec((hist, bt, ch), lambda i: (0, i, 0))
    blocks = [((hist, bt, ch), F32), ((bt, ch), F32), ((width, ch), F32), ((bt, ch), BF16), ((hist, bt, ch), F32)]
    return pl.pallas_call(
        _sample_conv_kernel,
        grid=(bs // bt,),
        in_specs=[st, pl.BlockSpec((bt, ch), lambda i: (i, 0)), pl.BlockSpec((width, ch), lambda i: (0, 0)),
                  vec, vec, vec],
        out_specs=[pl.BlockSpec((bt, ch), lambda i: (i, 0)), st],
        out_shape=[jax.ShapeDtypeStruct((bs, ch), BF16), jax.ShapeDtypeStruct((hist, bs, ch), F32)],
        compiler_params=_params(("parallel",), blocks),
        name="sample_conv",
    )(state, u, conv_w, conv_b, g, b)


def _layer_tail(x, pe, h, o, c, lw, dims):
    aw, cd = dims["attn_width"], dims["conv_dim"]
    mix_in = _merge(h, o, c, lw["w_in"], 3 * aw + 2 * cd, lw["w_attn_out"], lw["w_conv_out"])
    x1, f = _out_proj(mix_in, lw["w_out"], x, lw["g_post_mix"], lw["g_pre_ffn"])
    hd = _ffn_up(f, lw["w_ffn_gate"], lw["w_ffn_up"])
    x2 = _ffn_down(hd, lw["w_ffn_down"], x1, lw["g_post_ffn"])
    return _ple(x2, pe, lw["w_ple_gate"], lw["w_ple_proj"])


_MATRICES = ("w_in", "w_attn_out", "w_conv_out", "w_out", "w_ffn_gate", "w_ffn_up", "w_ffn_down",
             "w_ple_proj", "w_ple_gate")
_VECTORS = ("g_pre_mix", "g_subln", "conv_b", "g_conv_norm", "b_conv_norm", "g_post_mix", "g_pre_ffn",
            "g_post_ffn", "lambda_q1", "lambda_k1", "lambda_q2", "lambda_k2")


def kernel(x_prompt, x_sample, cache_k, cache_v, state_conv, page_table, p_prompt, p_sample, g_pre_mix, w_in, lambda_q1, lambda_k1, lambda_q2, lambda_k2, g_subln, w_attn_out, conv_w, conv_b, g_conv_norm, b_conv_norm, w_conv_out, w_out, g_post_mix, g_pre_ffn, w_ffn_gate, w_ffn_up, w_ffn_down, g_post_ffn, w_ple_proj, w_ple_gate):
    weights = dict(g_pre_mix=g_pre_mix, w_in=w_in, lambda_q1=lambda_q1, lambda_k1=lambda_k1,
                   lambda_q2=lambda_q2, lambda_k2=lambda_k2, g_subln=g_subln, w_attn_out=w_attn_out,
                   conv_w=conv_w, conv_b=conv_b, g_conv_norm=g_conv_norm, b_conv_norm=b_conv_norm,
                   w_conv_out=w_conv_out, w_out=w_out, g_post_mix=g_post_mix, g_pre_ffn=g_pre_ffn,
                   w_ffn_gate=w_ffn_gate, w_ffn_up=w_ffn_up, w_ffn_down=w_ffn_down, g_post_ffn=g_post_ffn,
                   w_ple_proj=w_ple_proj, w_ple_gate=w_ple_gate)
    depth = w_in.shape[0]
    bp, t, d = x_prompt.shape
    bs, ts, _ = x_sample.shape
    assert ts == 1, "the sample group decodes one token per sequence"
    n_heads, e = cache_k.shape[3], cache_k.shape[4]
    dims = dict(attn_width=n_heads * e, conv_dim=conv_w.shape[2], attn_scale=(e // 2) ** -0.5 * LOG2E)
    slopes = 2.0 ** (-8.0 * jnp.arange(1, n_heads + 1, dtype=F32) / n_heads)

    yp = x_prompt.reshape(bp * t, d)
    ys = x_sample.reshape(bs, d)
    n_pool = cache_k.shape[1]
    pool_k = cache_k.reshape((depth * n_pool,) + cache_k.shape[2:])
    pool_v = cache_v.reshape((depth * n_pool,) + cache_v.shape[2:])
    outs = [[] for _ in range(6)]
    for i in range(depth):
        lam_init = 0.8 - 0.6 * math.exp(-0.3 * i)
        lw = {n: weights[n][i].astype(BF16) for n in _MATRICES}
        lw.update({n: weights[n][i][None, :] for n in _VECTORS})
        lw["conv_w"] = conv_w[i]
        lams = [lw[n] for n in ("lambda_q1", "lambda_k1", "lambda_q2", "lambda_k2")]
        ln = (lw["conv_b"], lw["g_conv_norm"], lw["b_conv_norm"])

        aw, cd = dims["attn_width"], dims["conv_dim"]
        hp, qkv, kp, vp, up = _in_proj(yp, lw["g_pre_mix"], lw["w_in"], aw, cd, e, dims["attn_scale"])
        hs, qkv_s, ksn, vsn, us = _in_proj(ys, lw["g_pre_mix"], lw["w_in"], aw, cd, e, dims["attn_scale"])
        op, os_ = _attention(qkv.reshape(3 * n_heads, bp, t, e), jnp.swapaxes(qkv_s[:n_heads], 0, 1), ksn, vsn,
                             pool_k, pool_v, page_table + i * n_pool, slopes, lams, lw["g_subln"], lam_init,
                             n_heads)
        up3 = up.reshape(bp, t, cd)
        cp_act = _prompt_conv(up3, lw["conv_w"], *ln).reshape(bp * t, cd)
        cp = up3[:, t - (conv_w.shape[1] - 1):, :]
        cs_act, csn = _sample_conv(jnp.swapaxes(state_conv[i], 0, 1), us, lw["conv_w"], *ln)
        csn = jnp.swapaxes(csn, 0, 1)
        yp = _layer_tail(yp, p_prompt[i].reshape(bp * t, -1), hp, op.reshape(bp * t, aw), cp_act, lw, dims)
        ys = _layer_tail(ys, p_sample[i].reshape(bs, -1), hs, os_, cs_act, lw, dims)
        for lst, val in zip(outs, (kp.reshape(bp, t, n_heads, e), vp.reshape(bp, t, n_heads, e), cp,
                                   ksn.reshape(bs, 1, n_heads, e), vsn.reshape(bs, 1, n_heads, e), csn)):
            lst.append(val)
    stack = (lambda o: o[0][None]) if depth == 1 else jnp.stack
    return (yp.reshape(bp, t, d), ys.reshape(bs, 1, d), *[stack(o) for o in outs])
```

```python
import functools
import math

import jax
import jax.numpy as jnp
from jax import lax
from jax.experimental import pallas as pl
from jax.experimental.pallas import tpu as pltpu

F32 = jnp.float32
BF16 = jnp.bfloat16

EPS = 1e-6
NEG_INF = -1e30

LANES = 128
SUBLANES = 8
VMEM_BYTES_V7X = 64 * 1024 * 1024
VMEM_CEILING = VMEM_BYTES_V7X - 8 * 1024 * 1024
VMEM_INTERNAL = 12 * 1024 * 1024

ROW_TILE = 1024
COL_TILE = 512
NORM_ROWS = 256
ATTN_TILE = 512
CONV_ROWS = 128

LOG2E = 1.4426950408889634
BF16_EXACT_INT = 256
BF16_PIECES = 3


def _tile(n, pref):
    if n <= pref:
        return n
    if pref < LANES:
        assert n % pref == 0, (n, pref)
        return pref
    t = (pref // LANES) * LANES
    while t > LANES and n % t:
        t -= LANES
    assert n % t == 0, (n, pref)
    return t


def _nbytes(shape, dtype):
    return math.prod(shape) * jnp.dtype(dtype).itemsize


def _params(semantics, blocks, scratch=()):
    need = 2 * sum(_nbytes(s, d) for s, d in blocks)
    need += sum(_nbytes(s, d) for s, d in scratch) + VMEM_INTERNAL
    return pltpu.CompilerParams(dimension_semantics=semantics,
                                vmem_limit_bytes=min(need, VMEM_CEILING))


def _dot(a, b):
    return jnp.dot(a, b, preferred_element_type=F32)


def _dot_nt(a, b):
    return lax.dot_general(a, b, (((1,), (1,)), ((), ())), preferred_element_type=F32)


def _rms(x, g):
    return x * lax.rsqrt(jnp.mean(x * x, axis=-1, keepdims=True) + EPS) * g


def _lambda(lq1, lk1, lq2, lk2, lam_init):
    s1 = jnp.sum(lq1[...] * lk1[...], axis=-1, keepdims=True)
    s2 = jnp.sum(lq2[...] * lk2[...], axis=-1, keepdims=True)
    return jnp.exp(s1) - jnp.exp(s2) + lam_init


def _in_proj_kernel(x_ref, g_ref, wa_ref, wb_ref, h_ref, q_ref, k_ref, v_ref, u_ref, *, scale, n_attn):
    j = pl.program_id(1)

    @pl.when(j == 0)
    def _():
        rows = _tile(x_ref.shape[0], NORM_ROWS)
        for r in range(0, x_ref.shape[0], rows):
            h_ref[pl.ds(r, rows), :] = _rms(x_ref[pl.ds(r, rows), :], g_ref[...]).astype(BF16)

    @pl.when(j < n_attn)
    def _():
        q_ref[...] = (_dot(h_ref[...], wa_ref[...]) * scale).astype(BF16)

    @pl.when((j >= n_attn) & (j < 2 * n_attn))
    def _():
        k_ref[...] = _dot(h_ref[...], wa_ref[...])

    @pl.when((j >= 2 * n_attn) & (j < 3 * n_attn))
    def _():
        v_ref[...] = _dot(h_ref[...], wa_ref[...])

    @pl.when(j >= 3 * n_attn)
    def _():
        h = h_ref[...]
        u_ref[...] = _dot(h, wa_ref[...]) * jax.nn.sigmoid(_dot(h, wb_ref[...]))


def _merge_kernel(h_ref, o_ref, c_ref, wga_ref, wgc_ref, wa_ref, wc_ref, m_ref):
    h = h_ref[...]
    a = _dot(o_ref[...], wa_ref[...])
    c = _dot(c_ref[...], wc_ref[...])
    m = jax.nn.sigmoid(_dot(h, wga_ref[...])) * a + jax.nn.sigmoid(_dot(h, wgc_ref[...])) * c
    m_ref[...] = m.astype(BF16)


def _out_kernel(m_ref, w_ref, x_ref, gpost_ref, gpre_ref, x1_ref, f_ref):
    rows = _tile(m_ref.shape[0], NORM_ROWS)
    for r in range(0, m_ref.shape[0], rows):
        sl = pl.ds(r, rows)
        x1 = x_ref[sl, :] + _rms(_dot(m_ref[sl, :], w_ref[...]), gpost_ref[...])
        x1_ref[sl, :] = x1
        f_ref[sl, :] = _rms(x1, gpre_ref[...]).astype(BF16)


def _ffn_up_kernel(f_ref, wg_ref, wu_ref, hd_ref):
    f = f_ref[...]
    hd_ref[...] = (jax.nn.silu(_dot(f, wg_ref[...])) * _dot(f, wu_ref[...])).astype(BF16)


def _ffn_down_kernel(hd_ref, w_ref, x1_ref, g_ref, x2_ref, y_ref):
    j = pl.program_id(1)
    tn = w_ref.shape[1]
    y_ref[:, pl.ds(pl.multiple_of(j * tn, tn), tn)] = _dot(hd_ref[...], w_ref[...])

    @pl.when(j == pl.num_programs(1) - 1)
    def _():
        rows = _tile(y_ref.shape[0], NORM_ROWS)
        for r in range(0, y_ref.shape[0], rows):
            sl = pl.ds(r, rows)
            x2_ref[sl, :] = x1_ref[sl, :] + _rms(y_ref[sl, :], g_ref[...])


def _ple_kernel(x2_ref, pe_ref, wg_ref, wp_ref, y_ref, x2b_ref):
    j = pl.program_id(1)
    tn = y_ref.shape[1]

    @pl.when(j == 0)
    def _():
        x2b_ref[...] = x2_ref[...].astype(BF16)

    gate = jax.nn.sigmoid(_dot(x2b_ref[...], wg_ref[...]))
    res = x2_ref[:, pl.ds(pl.multiple_of(j * tn, tn), tn)]
    y_ref[...] = res + gate * _dot(pe_ref[...].astype(BF16), wp_ref[...])


def _bf16_pieces(x):
    pieces = []
    for _ in range(BF16_PIECES):
        p = x.astype(BF16).astype(F32)
        pieces.append(p)
        x = x - p
    return pieces


def _alibi_lanes(lane0, lane, groups):
    out = jnp.zeros(lane.shape, F32)
    for g, pieces in enumerate(groups):
        for i, p in enumerate(pieces):
            out = jnp.where(lane == lane0 + g * BF16_PIECES + i, p, out)
    return out


def _attn_kernel(pt_ref, slopes_ref, lq1, lk1, lq2, lk2, gsub_ref, kx1_ref, kx2_ref, mask_ref, q_ref, k_ref,
                 v_ref, sig_ref, bias_ref, qd_ref, kn_ref, vn_ref, *refs, lam_init, dh, t, group):
    del pt_ref
    kc_refs, vc_refs = refs[:group], refs[group:2 * group]
    (o_ref, od_ref, k1_ref, k2_ref, vt_ref, sa_ref, sb_ref, p_ref, acc_ref,
     qs_ref, dm_ref, dl_ref, dacc_ref) = refs[2 * group:]
    seq, e = k_ref.shape[1], k_ref.shape[2]
    n_blocks = seq // t
    qi = pl.program_id(2)
    sigma = slopes_ref[pl.program_id(1)] * LOG2E
    n_heads = qd_ref.shape[1]
    cols = kc_refs[0].shape[1]
    ps = cols // n_heads
    dsig = sig_ref[...]

    @pl.when(qi == 0)
    def _():
        k = k_ref[0].astype(BF16)
        lane = lax.broadcasted_iota(jnp.int32, k.shape, 1)
        k1_ref[...] = jnp.where(lane < dh, k, kx1_ref[...])
        k2_ref[...] = jnp.where(lane >= dh, k, kx2_ref[...])
        vt_ref[...] = v_ref[0].T.astype(BF16)
        qd = qd_ref[0]
        lane = lax.broadcasted_iota(jnp.int32, qd.shape, 1)
        zero = jnp.zeros_like(qd)
        qs_ref[...] = jnp.concatenate([jnp.where(lane < dh, qd, zero), jnp.where(lane >= dh, qd, zero)], axis=0)
        dm_ref[...] = jnp.full_like(dm_ref, NEG_INF)
        dl_ref[...] = jnp.zeros_like(dl_ref)
        dacc_ref[...] = jnp.zeros_like(dacc_ref)

    lam = _lambda(lq1, lk1, lq2, lk2, lam_init)

    def queries(c):
        q = q_ref[0, pl.ds(c * t, t), :].astype(F32)
        lane = lax.broadcasted_iota(jnp.int32, (t, e), 1)
        qpos = (c * t + lax.broadcasted_iota(jnp.int32, (t, e), 0)).astype(F32)
        sig = jnp.full((t, e), sigma, F32)
        sig_p = _bf16_pieces(sig)
        groups = [sig_p, sig_p, _bf16_pieces(-sig * qpos)]
        return (jnp.where(lane < dh, q, _alibi_lanes(dh, lane, groups)).T.astype(BF16),
                jnp.where(lane >= dh, q, _alibi_lanes(0, lane, groups)).T.astype(BF16))

    def scores(s_ref, qs, kj):
        keys = pl.ds(kj * t, t)
        s_ref[:, pl.ds(0, t)] = _dot(k1_ref[keys, :], qs[0])
        s_ref[:, pl.ds(t, t)] = _dot(k2_ref[keys, :], qs[1])

    def update(s_ref, kj, m, l, first, masked):
        m_out, l_out, alphas = [], [], []
        for c in range(2 * t // LANES):
            lanes = pl.ds(c * LANES, LANES)
            sc = s_ref[:, lanes]
            if masked:
                sc = sc + mask_ref[:, lanes]
            mc = jnp.max(sc, axis=0, keepdims=True)
            if not first:
                mc = jnp.maximum(m[c], mc)
            p = jnp.exp2(sc - mc)
            lc = jnp.sum(p, axis=0, keepdims=True)
            if not first:
                alpha = jnp.exp2(m[c] - mc)
                lc = alpha * l[c] + lc
                alphas.append(alpha)
            p_ref[:, lanes] = p.astype(BF16)
            m_out.append(mc)
            l_out.append(lc)
        pv = _dot(vt_ref[:, pl.ds(kj * t, t)], p_ref[...])
        if first:
            acc_ref[...] = pv
        else:
            acc_ref[...] = jnp.concatenate(alphas, axis=1) * acc_ref[...] + pv
        return m_out, l_out

    def prompt_block(c):
        score_refs = (sa_ref, sb_ref)
        qs = queries(c)
        scores(score_refs[0], qs, 0)
        m = l = None
        for kj in range(c + 1):
            if kj < c:
                scores(score_refs[(kj + 1) % 2], qs, kj + 1)
            m, l = update(score_refs[kj % 2], kj, m, l, first=(kj == 0), masked=(kj == c))
        l_row = jnp.concatenate(l, axis=1)
        acc = acc_ref[...]
        o = (acc[:, :t] / l_row[:, :t] - lam * (acc[:, t:] / l_row[:, t:])).T
        o_ref[0, pl.ds(c * t, t), :] = (_rms(o, gsub_ref[...]) * (1.0 - lam_init)).astype(BF16)

    def sample_pages(c):
        shift = -dsig * float((n_blocks - c) * group * ps)
        qs = qs_ref[...]
        s = jnp.concatenate([_dot_nt(qs, kc[0].astype(BF16)) for kc in kc_refs], axis=1) + bias_ref[...]
        m_old = dm_ref[...]
        m_new = jnp.maximum(m_old, jnp.max(s, axis=-1, keepdims=True) + shift)
        p = jnp.exp2(s - (m_new - shift))
        alpha = jnp.exp2(m_old - m_new)
        dm_ref[...] = m_new
        dl_ref[...] = alpha * dl_ref[...] + jnp.sum(p, axis=-1, keepdims=True)
        p = p.astype(BF16)
        pv = _dot(p[:, :cols], vc_refs[0][0].astype(BF16))
        for g in range(1, group):
            pv = pv + _dot(p[:, g * cols:(g + 1) * cols], vc_refs[g][0].astype(BF16))
        dacc_ref[...] = alpha * dacc_ref[...] + pv

    for c in range(n_blocks):
        @pl.when(qi == c)
        def _(c=c):
            sample_pages(c)
            prompt_block(c)

    @pl.when(qi == n_blocks - 1)
    def _():
        prod = qd_ref[0].astype(F32) * kn_ref[0].astype(BF16).astype(F32)
        lane = lax.broadcasted_iota(jnp.int32, prod.shape, 1)
        s1 = jnp.sum(jnp.where(lane < dh, prod, 0.0), axis=-1, keepdims=True)
        s2 = jnp.sum(jnp.where(lane >= dh, prod, 0.0), axis=-1, keepdims=True)
        s_self = jnp.concatenate([s1, s2], axis=0)
        vn = vn_ref[0].astype(BF16).astype(F32)
        m_old = dm_ref[...]
        m_fin = jnp.maximum(m_old, s_self)
        alpha = jnp.exp2(m_old - m_fin)
        p_self = jnp.exp2(s_self - m_fin)
        l = alpha * dl_ref[...] + p_self
        acc = alpha * dacc_ref[...] + p_self * jnp.concatenate([vn, vn], axis=0)
        a = acc / l
        o = a[:n_heads] - lam * a[n_heads:]
        od_ref[0] = (_rms(o, gsub_ref[...]) * (1.0 - lam_init)).astype(BF16)


def _ln_silu(c, g, b):
    mu = jnp.mean(c, axis=-1, keepdims=True)
    var = jnp.mean(jnp.square(c - mu), axis=-1, keepdims=True)
    y = (c - mu) * lax.rsqrt(var + EPS) * g + b
    return jax.nn.silu(y)


def _prompt_conv_kernel(prev_ref, cur_ref, w_ref, cb_ref, g_ref, b_ref, o_ref, full_ref, c_ref, sh_ref):
    hist = prev_ref.shape[1]
    tt, ch = c_ref.shape
    width = w_ref.shape[0]

    @pl.when(pl.program_id(1) == 0)
    def _():
        full_ref[pl.ds(0, hist), :] = jnp.zeros((hist, ch), F32)

    @pl.when(pl.program_id(1) > 0)
    def _():
        full_ref[pl.ds(0, hist), :] = prev_ref[0]

    full_ref[pl.ds(hist, tt), :] = cur_ref[0]

    base = hist - (width - 1)

    def chunk(cc, _):
        lanes = pl.ds(pl.multiple_of(cc * LANES, LANES), LANES)
        acc = jnp.broadcast_to(cb_ref[:, lanes], (tt, LANES))
        for r in range(min(SUBLANES, width)):
            taps = range(r, width, SUBLANES)
            span = tt + taps[-1] - r
            sh_ref[r, pl.ds(0, span), :] = full_ref[pl.ds(base + r, span), lanes]
            for k in taps:
                acc = acc + w_ref[pl.ds(k, 1), lanes] * sh_ref[r, pl.ds(k - r, tt), :]
        c_ref[:, lanes] = acc
        return 0

    lax.fori_loop(0, ch // LANES, chunk, 0)
    o_ref[0] = _ln_silu(c_ref[...], g_ref[...], b_ref[...]).astype(BF16)


def _sample_conv_kernel(state_ref, u_ref, w_ref, cb_ref, g_ref, b_ref, o_ref, ns_ref):
    hist = state_ref.shape[0]
    u = u_ref[...]
    c = u * w_ref[pl.ds(hist, 1), :] + cb_ref[...]
    for k in range(hist):
        c = c + state_ref[k] * w_ref[pl.ds(k, 1), :]
    o_ref[...] = _ln_silu(c, g_ref[...], b_ref[...]).astype(BF16)
    for k in range(hist - 1):
        ns_ref[k] = state_ref[k + 1]
    ns_ref[hist - 1] = u


def _row_spec(tm, n):
    return pl.BlockSpec((tm, n), lambda i, j: (i, 0))


def _col_spec(k, tn, off_blocks=0):
    return pl.BlockSpec((k, tn), lambda i, j: (0, j + off_blocks))


def _tile_spec(tm, tn):
    return pl.BlockSpec((tm, tn), lambda i, j: (i, j))


def _vec_spec(n):
    return pl.BlockSpec((1, n), lambda i, j: (0, 0))


def _in_proj(x, g, w_in, aw, cd, scale):
    m, d = x.shape
    tm, tn = _tile(m, ROW_TILE), _tile(math.gcd(aw, cd), COL_TILE)
    na, nu = aw // tn, cd // tn
    gate0 = (3 * aw + cd) // tn

    def out_tile(first, count):
        return pl.BlockSpec((tm, tn), lambda i, j: (i, jnp.clip(j - first, 0, count - 1)))

    blocks = [((tm, d), F32), ((1, d), F32), ((d, tn), BF16), ((d, tn), BF16), ((tm, d), BF16),
              ((tm, tn), BF16), ((tm, tn), F32), ((tm, tn), F32), ((tm, tn), F32)]
    return pl.pallas_call(
        functools.partial(_in_proj_kernel, scale=scale, n_attn=na),
        grid=(m // tm, 3 * na + nu),
        in_specs=[_row_spec(tm, d), _vec_spec(d), _col_spec(d, tn),
                  pl.BlockSpec((d, tn), lambda i, j: (0, gate0 + jnp.maximum(j - 3 * na, 0)))],
        out_specs=[_row_spec(tm, d), out_tile(0, na), out_tile(na, na), out_tile(2 * na, na),
                   out_tile(3 * na, nu)],
        out_shape=[jax.ShapeDtypeStruct((m, d), BF16), jax.ShapeDtypeStruct((m, aw), BF16),
                   jax.ShapeDtypeStruct((m, aw), F32), jax.ShapeDtypeStruct((m, aw), F32),
                   jax.ShapeDtypeStruct((m, cd), F32)],
        compiler_params=_params(("parallel", "arbitrary"), blocks),
        name="in_proj",
    )(x, g, w_in, w_in)


def _merge(h, o, c, w_in, off, w_attn_out, w_conv_out):
    m, d = h.shape
    aw, cd = o.shape[1], c.shape[1]
    tm, tn = _tile(m, ROW_TILE), _tile(d, COL_TILE)
    assert off % tn == 0
    blocks = [((tm, d), BF16), ((tm, aw), BF16), ((tm, cd), BF16), ((d, tn), BF16), ((d, tn), BF16),
              ((aw, tn), BF16), ((cd, tn), BF16), ((tm, tn), BF16)]
    return pl.pallas_call(
        _merge_kernel,
        grid=(m // tm, d // tn),
        in_specs=[_row_spec(tm, d), _row_spec(tm, aw), _row_spec(tm, cd),
                  _col_spec(d, tn, off // tn), _col_spec(d, tn, (off + d) // tn),
                  _col_spec(aw, tn), _col_spec(cd, tn)],
        out_specs=_tile_spec(tm, tn),
        out_shape=jax.ShapeDtypeStruct((m, d), BF16),
        compiler_params=_params(("parallel", "arbitrary"), blocks),
        name="gated_merge",
    )(h, o, c, w_in, w_in, w_attn_out, w_conv_out)


def _out_proj(mix_in, w_out, x, g_post, g_pre):
    m, d = x.shape
    tm = _tile(m, ROW_TILE // 2)
    row = pl.BlockSpec((tm, d), lambda i: (i, 0))
    vec = pl.BlockSpec((1, d), lambda i: (0, 0))
    blocks = [((tm, d), BF16), ((d, d), BF16), ((tm, d), F32), ((tm, d), F32), ((tm, d), BF16)]
    return pl.pallas_call(
        _out_kernel,
        grid=(m // tm,),
        in_specs=[row, pl.BlockSpec((d, d), lambda i: (0, 0)), row, vec, vec],
        out_specs=[row, row],
        out_shape=[jax.ShapeDtypeStruct((m, d), F32), jax.ShapeDtypeStruct((m, d), BF16)],
        compiler_params=_params(("parallel",), blocks),
        name="out_proj",
    )(mix_in, w_out, x, g_post, g_pre)


def _ffn_up(f, w_gate, w_up):
    m, d = f.shape
    dff = w_gate.shape[1]
    tm, tn = _tile(m, ROW_TILE), _tile(dff, COL_TILE)
    blocks = [((tm, d), BF16), ((d, tn), BF16), ((d, tn), BF16), ((tm, tn), BF16)]
    return pl.pallas_call(
        _ffn_up_kernel,
        grid=(m // tm, dff // tn),
        in_specs=[_row_spec(tm, d), _col_spec(d, tn), _col_spec(d, tn)],
        out_specs=_tile_spec(tm, tn),
        out_shape=jax.ShapeDtypeStruct((m, dff), BF16),
        compiler_params=_params(("parallel", "arbitrary"), blocks),
        name="ffn_up",
    )(f, w_gate, w_up)


def _ffn_down(hd, w_down, x1, g_post):
    m, dff = hd.shape
    d = x1.shape[1]
    tm, tn = _tile(m, ROW_TILE // 2), _tile(d, COL_TILE)
    row = pl.BlockSpec((tm, d), lambda i, j: (i, 0))
    blocks = [((tm, dff), BF16), ((dff, tn), BF16), ((tm, d), F32), ((tm, d), F32)]
    scratch = [((tm, d), F32)]
    return pl.pallas_call(
        _ffn_down_kernel,
        grid=(m // tm, d // tn),
        in_specs=[_row_spec(tm, dff), _col_spec(dff, tn), row, _vec_spec(d)],
        out_specs=row,
        out_shape=jax.ShapeDtypeStruct((m, d), F32),
        scratch_shapes=[pltpu.VMEM(s, dt) for s, dt in scratch],
        compiler_params=_params(("parallel", "arbitrary"), blocks, scratch),
        name="ffn_down",
    )(hd, w_down, x1, g_post)


def _ple(x2, pe, w_gate, w_proj):
    m, d = x2.shape
    pd = pe.shape[1]
    tm, tn = _tile(m, ROW_TILE), _tile(d, 2 * COL_TILE)
    blocks = [((tm, d), F32), ((tm, pd), F32), ((d, tn), BF16), ((pd, tn), BF16), ((tm, tn), F32)]
    scratch = [((tm, d), BF16)]
    return pl.pallas_call(
        _ple_kernel,
        grid=(m // tm, d // tn),
        in_specs=[_row_spec(tm, d), _row_spec(tm, pd), _col_spec(d, tn), _col_spec(pd, tn)],
        out_specs=_tile_spec(tm, tn),
        out_shape=jax.ShapeDtypeStruct((m, d), F32),
        scratch_shapes=[pltpu.VMEM(s, dt) for s, dt in scratch],
        compiler_params=_params(("parallel", "arbitrary"), blocks, scratch),
        name="ple",
    )(x2, pe, w_gate, w_proj)


def _lambda_specs(dh, index_map):
    return [pl.BlockSpec((1, dh), index_map)] * 4


def _alibi_key_lanes(t, e, lane0):
    pos = jnp.arange(t, dtype=jnp.int32)[:, None]
    lo = pos % BF16_EXACT_INT
    g = (jnp.arange(e, dtype=jnp.int32)[None, :] - lane0) // BF16_PIECES
    vals = jnp.where(g == 0, pos - lo, jnp.where(g == 1, lo, 1))
    return jnp.where((g >= 0) & (g < 3), vals, 0).astype(BF16)


def _attention(q, k, v, qd, kd, vd, cache_k, cache_v, page_table, slopes, lams, g_subln, lam_init, n_heads):
    b, t, aw = q.shape
    bs = qd.shape[0]
    e = aw // n_heads
    dh = e // 2
    assert e == LANES and dh >= 3 * BF16_PIECES, "ALiBi lanes ride in the unused half of each map's copy"
    assert bs == b * n_heads, "one sample sequence rides on each (batch, head) of the prompt grid"
    tq = _tile(t, ATTN_TILE)
    nq = t // tq
    n_pool, ps = cache_k.shape[:2]
    n_pages = page_table.shape[1]
    assert n_pages % nq == 0, "each query-block step takes an equal share of the pages"
    group = n_pages // nq
    rows, cols = 2 * n_heads, ps * n_heads
    key = jnp.arange(tq, dtype=jnp.int32)[:, None]
    qry = jnp.arange(2 * tq, dtype=jnp.int32)[None, :] % tq
    mask = jnp.where(key <= qry, 0.0, NEG_INF).astype(F32)
    col = jnp.arange(group * cols, dtype=jnp.int32)[None, :]
    sig_rows = (jnp.tile(slopes, 2) * LOG2E)[:, None]
    own_head = (col % n_heads) == (jnp.arange(rows, dtype=jnp.int32)[:, None] % n_heads)
    bias = jnp.where(own_head, sig_rows * (col // n_heads).astype(F32), NEG_INF)

    const = lambda bi, h, qi, pt: (0, 0)
    seq = pl.BlockSpec((1, t, e), lambda bi, h, qi, pt: (bi, 0, h))
    tok = pl.BlockSpec((1, n_heads, e), lambda bi, h, qi, pt: (bi * n_heads + h, 0, 0))
    pages = [pl.BlockSpec((1, cols, e),
                          lambda bi, h, qi, pt, g=g: (pt[(bi * n_heads + h) * n_pages + qi * group + g], 0, 0))
             for g in range(group)]
    blocks = [((t, e), F32), ((t, e), F32), ((t, e), BF16), ((t, e), BF16), ((t, e), BF16), ((t, e), BF16),
              ((tq, 2 * tq), F32), ((rows, group * cols), F32)] + [((cols, e), F32)] * (2 * group)
    scratch = [((t, e), BF16), ((t, e), BF16), ((e, t), BF16), ((tq, 2 * tq), F32), ((tq, 2 * tq), F32),
               ((tq, 2 * tq), BF16), ((e, 2 * tq), F32),
               ((rows, e), BF16), ((rows, 1), F32), ((rows, 1), F32), ((rows, e), F32)]
    grid_spec = pltpu.PrefetchScalarGridSpec(
        num_scalar_prefetch=1,
        grid=(b, n_heads, nq),
        in_specs=[pl.BlockSpec(memory_space=pltpu.SMEM)] + _lambda_specs(dh, const)
        + [pl.BlockSpec((1, e), const), pl.BlockSpec((t, e), const), pl.BlockSpec((t, e), const),
           pl.BlockSpec((tq, 2 * tq), const), seq, seq, seq,
           pl.BlockSpec((rows, 1), const), pl.BlockSpec((rows, group * cols), const), tok, tok, tok]
        + pages + pages,
        out_specs=[seq, tok],
        scratch_shapes=[pltpu.VMEM(s, d) for s, d in scratch],
    )
    pool_k = cache_k.reshape(n_pool, cols, e)
    pool_v = cache_v.reshape(n_pool, cols, e)
    o, od = pl.pallas_call(
        functools.partial(_attn_kernel, lam_init=lam_init, dh=dh, t=tq, group=group),
        grid_spec=grid_spec,
        out_shape=[jax.ShapeDtypeStruct((b, t, aw), BF16), jax.ShapeDtypeStruct((bs, n_heads, e), BF16)],
        compiler_params=_params(("parallel", "parallel", "arbitrary"), blocks, scratch),
        name="attention",
    )(page_table.reshape(-1), slopes, *lams, g_subln, _alibi_key_lanes(t, e, dh), _alibi_key_lanes(t, e, 0), mask,
      q, k, v, sig_rows, bias,
      qd.reshape(bs, n_heads, e), kd.reshape(bs, n_heads, e), vd.reshape(bs, n_heads, e),
      *([pool_k] * group), *([pool_v] * group))
    return o, od.reshape(bs, aw)


def _prompt_conv(u, conv_w, conv_b, g, b):
    bsz, t, ch = u.shape
    width = conv_w.shape[0]
    hist = -(-(width - 1) // SUBLANES) * SUBLANES
    tt = _tile(t, CONV_ROWS)
    assert tt % hist == 0
    const = lambda bi, ti: (0, 0)
    vec = pl.BlockSpec((1, ch), const)
    blocks = [((hist, ch), F32), ((tt, ch), F32), ((width, ch), F32), ((tt, ch), BF16)]
    scratch = [((hist + tt, ch), F32), ((tt, ch), F32), ((SUBLANES, hist + tt, LANES), F32)]
    return pl.pallas_call(
        _prompt_conv_kernel,
        grid=(bsz, t // tt),
        in_specs=[pl.BlockSpec((1, hist, ch), lambda bi, ti: (bi, jnp.maximum(ti * (tt // hist) - 1, 0), 0)),
                  pl.BlockSpec((1, tt, ch), lambda bi, ti: (bi, ti, 0)),
                  pl.BlockSpec((width, ch), const), vec, vec, vec],
        out_specs=pl.BlockSpec((1, tt, ch), lambda bi, ti: (bi, ti, 0)),
        out_shape=jax.ShapeDtypeStruct((bsz, t, ch), BF16),
        scratch_shapes=[pltpu.VMEM(s, d) for s, d in scratch],
        compiler_params=_params(("parallel", "arbitrary"), blocks, scratch),
        name="prompt_conv",
    )(u, u, conv_w, conv_b, g, b)


def _sample_conv(state, u, conv_w, conv_b, g, b):
    hist, bs, ch = state.shape
    width = conv_w.shape[0]
    assert hist == width - 1
    bt = _tile(bs, 4 * SUBLANES) if bs % SUBLANES == 0 else bs
    vec = pl.BlockSpec((1, ch), lambda i: (0, 0))
    st = pl.BlockSpec((hist, bt, ch), lambda i: (0, i, 0))
    blocks = [((hist, bt, ch), F32), ((bt, ch), F32), ((width, ch), F32), ((bt, ch), BF16), ((hist, bt, ch), F32)]
    return pl.pallas_call(
        _sample_conv_kernel,
        grid=(bs // bt,),
        in_specs=[st, pl.BlockSpec((bt, ch), lambda i: (i, 0)), pl.BlockSpec((width, ch), lambda i: (0, 0)),
                  vec, vec, vec],
        out_specs=[pl.BlockSpec((bt, ch), lambda i: (i, 0)), st],
        out_shape=[jax.ShapeDtypeStruct((bs, ch), BF16), jax.ShapeDtypeStruct((hist, bs, ch), F32)],
        compiler_params=_params(("parallel",), blocks),
        name="sample_conv",
    )(state, u, conv_w, conv_b, g, b)


def _layer_tail(x, pe, h, o, c, lw, dims):
    aw, cd = dims["attn_width"], dims["conv_dim"]
    mix_in = _merge(h, o, c, lw["w_in"], 3 * aw + 2 * cd, lw["w_attn_out"], lw["w_conv_out"])
    x1, f = _out_proj(mix_in, lw["w_out"], x, lw["g_post_mix"], lw["g_pre_ffn"])
    hd = _ffn_up(f, lw["w_ffn_gate"], lw["w_ffn_up"])
    x2 = _ffn_down(hd, lw["w_ffn_down"], x1, lw["g_post_ffn"])
    return _ple(x2, pe, lw["w_ple_gate"], lw["w_ple_proj"])


_MATRICES = ("w_in", "w_attn_out", "w_conv_out", "w_out", "w_ffn_gate", "w_ffn_up", "w_ffn_down",
             "w_ple_proj", "w_ple_gate")
_VECTORS = ("g_pre_mix", "g_subln", "conv_b", "g_conv_norm", "b_conv_norm", "g_post_mix", "g_pre_ffn",
            "g_post_ffn", "lambda_q1", "lambda_k1", "lambda_q2", "lambda_k2")


def kernel(x_prompt, x_sample, cache_k, cache_v, state_conv, page_table, p_prompt, p_sample, g_pre_mix, w_in, lambda_q1, lambda_k1, lambda_q2, lambda_k2, g_subln, w_attn_out, conv_w, conv_b, g_conv_norm, b_conv_norm, w_conv_out, w_out, g_post_mix, g_pre_ffn, w_ffn_gate, w_ffn_up, w_ffn_down, g_post_ffn, w_ple_proj, w_ple_gate):
    weights = dict(g_pre_mix=g_pre_mix, w_in=w_in, lambda_q1=lambda_q1, lambda_k1=lambda_k1,
                   lambda_q2=lambda_q2, lambda_k2=lambda_k2, g_subln=g_subln, w_attn_out=w_attn_out,
                   conv_w=conv_w, conv_b=conv_b, g_conv_norm=g_conv_norm, b_conv_norm=b_conv_norm,
                   w_conv_out=w_conv_out, w_out=w_out, g_post_mix=g_post_mix, g_pre_ffn=g_pre_ffn,
                   w_ffn_gate=w_ffn_gate, w_ffn_up=w_ffn_up, w_ffn_down=w_ffn_down, g_post_ffn=g_post_ffn,
                   w_ple_proj=w_ple_proj, w_ple_gate=w_ple_gate)
    depth = w_in.shape[0]
    bp, t, d = x_prompt.shape
    bs, ts, _ = x_sample.shape
    assert ts == 1, "the sample group decodes one token per sequence"
    n_heads, e = cache_k.shape[3], cache_k.shape[4]
    dims = dict(attn_width=n_heads * e, conv_dim=conv_w.shape[2], attn_scale=(e // 2) ** -0.5 * LOG2E)
    slopes = 2.0 ** (-8.0 * jnp.arange(1, n_heads + 1, dtype=F32) / n_heads)

    yp = x_prompt.reshape(bp * t, d)
    ys = x_sample.reshape(bs, d)
    n_pool = cache_k.shape[1]
    pool_k = cache_k.reshape((depth * n_pool,) + cache_k.shape[2:])
    pool_v = cache_v.reshape((depth * n_pool,) + cache_v.shape[2:])
    outs = [[] for _ in range(6)]
    for i in range(depth):
        lam_init = 0.8 - 0.6 * math.exp(-0.3 * i)
        lw = {n: weights[n][i].astype(BF16) for n in _MATRICES}
        lw.update({n: weights[n][i][None, :] for n in _VECTORS})
        lw["conv_w"] = conv_w[i]
        lams = [lw[n] for n in ("lambda_q1", "lambda_k1", "lambda_q2", "lambda_k2")]
        ln = (lw["conv_b"], lw["g_conv_norm"], lw["b_conv_norm"])

        aw, cd = dims["attn_width"], dims["conv_dim"]
        hp, qp, kp, vp, up = _in_proj(yp, lw["g_pre_mix"], lw["w_in"], aw, cd, dims["attn_scale"])
        hs, qs, ksn, vsn, us = _in_proj(ys, lw["g_pre_mix"], lw["w_in"], aw, cd, dims["attn_scale"])
        shape = (bp, t, aw)
        op, os_ = _attention(qp.reshape(shape), kp.reshape(shape), vp.reshape(shape), qs, ksn, vsn, pool_k, pool_v,
                             page_table + i * n_pool, slopes, lams, lw["g_subln"], lam_init, n_heads)
        up3 = up.reshape(bp, t, cd)
        cp_act = _prompt_conv(up3, lw["conv_w"], *ln).reshape(bp * t, cd)
        cp = up3[:, t - (conv_w.shape[1] - 1):, :]
        cs_act, csn = _sample_conv(jnp.swapaxes(state_conv[i], 0, 1), us, lw["conv_w"], *ln)
        csn = jnp.swapaxes(csn, 0, 1)
        yp = _layer_tail(yp, p_prompt[i].reshape(bp * t, -1), hp, op.reshape(bp * t, aw), cp_act, lw, dims)
        ys = _layer_tail(ys, p_sample[i].reshape(bs, -1), hs, os_, cs_act, lw, dims)
        for lst, val in zip(outs, (kp.reshape(bp, t, n_heads, e), vp.reshape(bp, t, n_heads, e), cp,
                                   ksn.reshape(bs, 1, n_heads, e), vsn.reshape(bs, 1, n_heads, e), csn)):
            lst.append(val)
    stack = (lambda o: o[0][None]) if depth == 1 else jnp.stack
    return (yp.reshape(bp, t, d), ys.reshape(bs, 1, d), *[stack(o) for o in outs])
```
